```python
import jax, jax.numpy as jnp
from jax import lax
import numpy as np

D_MODEL = 2048
BATCH = 2
SEQ = 8192
DEPTH = 1

GRID_W = 64
CTX_LEN = 256
NA_HEADS = 16
NA_HEAD_DIM = 64
NA_W = NA_HEADS * NA_HEAD_DIM
NA_WIN_ROWS = 8
NA_WIN_COLS = 16
RET_HEADS = 8
RET_QK_DIM = 128
RET_V_DIM = 256
RET_QK_W = RET_HEADS * RET_QK_DIM
RET_V_W = RET_HEADS * RET_V_DIM
RET_CHUNK = 128
D_FF = 5632
N_MOD = 9
ROPE_BASE = 10000.0
LN_EPS = 1e-5
GN_EPS = 1e-6
NEG_INF = -1e30
ALPHA = (2 * DEPTH) ** 0.25
BETA = (8 * DEPTH) ** -0.25
COL_SPLITS = (NA_W, NA_W, RET_QK_W, RET_V_W, NA_W, RET_QK_W, RET_V_W, D_MODEL, D_MODEL)
KV_COLS = 2 * NA_W + RET_QK_W + RET_V_W
MIX_COLS = sum(COL_SPLITS)

kernel_name = "hybrid_natten_retention_macaron_deepnorm"


def layer_norm(x, g, b):
    xf = x.astype(jnp.float32)
    mu = jnp.mean(xf, axis=-1, keepdims=True)
    var = jnp.mean(jnp.square(xf - mu), axis=-1, keepdims=True)
    return ((xf - mu) * lax.rsqrt(var + LN_EPS) * g + b).astype(x.dtype)


def head_group_norm(o):
    of = o.astype(jnp.float32)
    mu = jnp.mean(of, axis=-1, keepdims=True)
    var = jnp.mean(jnp.square(of - mu), axis=-1, keepdims=True)
    return ((of - mu) * lax.rsqrt(var + GN_EPS)).astype(o.dtype)


def modulate(x, shift, scale):
    return x * (1.0 + scale) + shift


def swiglu(h, w_in, w_out):
    gate, up = jnp.split(h @ w_in, 2, axis=-1)
    return (jax.nn.silu(gate) * up) @ w_out


def split_cols(p, sizes):
    offsets, acc = [], 0
    for s in sizes[:-1]:
        acc += s
        offsets.append(acc)
    return jnp.split(p, offsets, axis=-1)


def to_heads(t, n_heads):
    return t.reshape(*t.shape[:-1], n_heads, t.shape[-1] // n_heads)


def axial_rope(x, prow, pcol):
    nfreq = x.shape[-1] // 4
    freqs = ROPE_BASE ** (-jnp.arange(nfreq, dtype=jnp.float32) / nfreq)
    ang = jnp.stack([prow, pcol], axis=-1)[:, :, None] * freqs
    cos = jnp.cos(ang)[None, :, None].astype(x.dtype)
    sin = jnp.sin(ang)[None, :, None].astype(x.dtype)
    xr = x.reshape(*x.shape[:-1], 2, 2, nfreq)
    a, b = xr[..., 0, :], xr[..., 1, :]
    return jnp.stack([a * cos - b * sin, a * sin + b * cos], axis=-2).reshape(x.shape)


def neighbourhood_attention(q, k, v, kc, vc, rpb):
    batch, length, n_heads, dh = q.shape
    rows = length // GRID_W
    wr = min(NA_WIN_ROWS, rows)
    wc = NA_WIN_COLS
    qg = (q * dh ** -0.5).reshape(batch, rows, GRID_W, n_heads, dh)
    kg = k.reshape(batch, rows, GRID_W, n_heads, dh)
    vg = v.reshape(batch, rows, GRID_W, n_heads, dh)
    col = jnp.arange(GRID_W)
    col_start = jnp.clip(col - wc // 2, 0, GRID_W - wc)
    col_mask = (col[None, :] >= col_start[:, None]) & (col[None, :] < col_start[:, None] + wc)
    dc_idx = jnp.clip(col[None, :] - col[:, None] + wc - 1, 0, 2 * wc - 2)

    def one_row(r):
        rs = jnp.clip(r - wr // 2, 0, rows - wr)
        q_row = lax.dynamic_index_in_dim(qg, r, axis=1, keepdims=False)
        k_win = lax.dynamic_slice_in_dim(kg, rs, wr, axis=1)
        v_win = lax.dynamic_slice_in_dim(vg, rs, wr, axis=1)
        dr_idx = rs + jnp.arange(wr) - r + NA_WIN_ROWS - 1
        bias = rpb[:, dr_idx][:, :, dc_idx]
        s_lat = jnp.einsum('bqhd,bikhd->bhqik', q_row, k_win).astype(jnp.float32)
        s_lat = s_lat + bias.transpose(0, 2, 1, 3).astype(jnp.float32)
        s_lat = jnp.where(col_mask[:, None, :], s_lat, NEG_INF)
        s_ctx = jnp.einsum('bqhd,bchd->bhqc', q_row, kc).astype(jnp.float32)
        s = jnp.concatenate([s_lat.reshape(batch, n_heads, GRID_W, wr * GRID_W), s_ctx], axis=-1)
        p = jax.nn.softmax(s, axis=-1)
        p_lat = p[..., : wr * GRID_W].reshape(batch, n_heads, GRID_W, wr, GRID_W).astype(v.dtype)
        p_ctx = p[..., wr * GRID_W:].astype(v.dtype)
        return (jnp.einsum('bhqik,bikhd->bqhd', p_lat, v_win)
                + jnp.einsum('bhqc,bchd->bqhd', p_ctx, vc))

    out = lax.map(one_row, jnp.arange(rows))
    return jnp.moveaxis(out, 0, 1).reshape(batch, length, n_heads, dh)


def context_attention(q, k, v):
    s = jnp.einsum('bqhd,bkhd->bhqk', q, k).astype(jnp.float32) * q.shape[-1] ** -0.5
    p = jax.nn.softmax(s, axis=-1).astype(v.dtype)
    return jnp.einsum('bhqk,bkhd->bqhd', p, v)


def retention_chunked(q, k, v, log_gamma, s0, strict):
    batch, length, n_heads, dk = q.shape
    dv = v.shape[-1]
    n = length // RET_CHUNK
    qc = q.reshape(batch, n, RET_CHUNK, n_heads, dk)
    kc = k.reshape(batch, n, RET_CHUNK, n_heads, dk)
    vc = v.reshape(batch, n, RET_CHUNK, n_heads, dv)
    pos = jnp.arange(RET_CHUNK, dtype=jnp.float32)
    diff = pos[:, None] - pos[None, :]
    mask = (diff > 0) if strict else (diff >= 0)
    decay = jnp.where(mask[None], jnp.exp(log_gamma[:, None, None] * jnp.where(mask, diff, 0.0)[None]), 0.0)
    scores = jnp.einsum('bnihd,bnjhd->bnhij', qc, kc) * decay[None, None]
    o_intra = jnp.einsum('bnhij,bnjhe->bnihe', scores, vc)
    zeta = jnp.exp(log_gamma[:, None] * (RET_CHUNK - 1 - pos))
    xi = jnp.exp(log_gamma[:, None] * (pos + 1.0))
    chunk_kv = jnp.einsum('bnjhd,hj,bnjhe->nbhde', kc, zeta, vc)
    g_chunk = jnp.exp(log_gamma * RET_CHUNK)[None, :, None, None]

    def step(s, kv):
        return g_chunk * s + kv, s

    _, s_prev = lax.scan(step, s0.astype(chunk_kv.dtype), chunk_kv)
    o_cross = jnp.einsum('bnihd,nbhde->bnihe', qc, s_prev) * xi.T[None, None, :, :, None]
    return (o_intra + o_cross).reshape(batch, length, n_heads, dv)


def bidirectional_retention(q, k, v, log_gf, log_gb, s_f, s_b):
    o_f = retention_chunked(q, k, v, log_gf, s_f, strict=False)
    flip = lambda t: jnp.flip(t, axis=1)
    o_b = flip(retention_chunked(flip(q), flip(k), flip(v), log_gb, s_b, strict=True))
    return o_f + o_b


def context_retention_states(k, v, log_gf, log_gb):
    n_ctx = k.shape[1]
    pos = jnp.arange(n_ctx, dtype=jnp.float32)
    w_f = jnp.exp(log_gf[:, None] * (n_ctx - 1 - pos))
    w_b = jnp.exp(log_gb[:, None] * pos)
    s_f = jnp.einsum('bchd,hc,bche->bhde', k, w_f, v)
    s_b = jnp.einsum('bchd,hc,bche->bhde', k, w_b, v)
    return s_f, s_b


def merge_branches(o_na, o_ret, g_ret, gate_na, gate_ret, w_na_out, w_ret_out, w_out):
    y_na = o_na.reshape(*o_na.shape[:-2], NA_W) @ w_na_out
    y_ret = (jax.nn.silu(g_ret) * head_group_norm(o_ret).reshape(*o_ret.shape[:-2], RET_V_W)) @ w_ret_out
    return (jax.nn.sigmoid(gate_na) * y_na + jax.nn.sigmoid(gate_ret) * y_ret) @ w_out


def token_mixers(h, hc, w_in, rpb, decay_f, decay_b, w_na_out, w_ret_out, w_out, with_ctx_out):
    length = h.shape[1]
    log_gf = jax.nn.log_sigmoid(decay_f.astype(jnp.float32))
    log_gb = jax.nn.log_sigmoid(decay_b.astype(jnp.float32))
    k_a, v_a, k_r, v_r, q_a, q_r, g_r, gate_na, gate_ret = split_cols(h @ w_in, COL_SPLITS)
    n_parts = len(COL_SPLITS) if with_ctx_out else 4
    n_cols = MIX_COLS if with_ctx_out else KV_COLS
    ctx_parts = split_cols(hc @ w_in[:, :n_cols], COL_SPLITS[:n_parts])
    kc_a, vc_a, kc_r, vc_r = ctx_parts[:4]
    pos = jnp.arange(length)
    prow = (pos // GRID_W).astype(jnp.float32)
    pcol = (pos % GRID_W).astype(jnp.float32)
    k_scale = RET_QK_DIM ** -0.5
    kcr = to_heads(kc_r, RET_HEADS) * k_scale
    vcr = to_heads(vc_r, RET_HEADS)
    s_f, s_b = context_retention_states(kcr, vcr, log_gf, log_gb)
    qr = axial_rope(to_heads(q_r, RET_HEADS), prow, pcol)
    kr = axial_rope(to_heads(k_r, RET_HEADS), prow, pcol) * k_scale
    o_ret = bidirectional_retention(qr, kr, to_heads(v_r, RET_HEADS), log_gf, log_gb, s_f, s_b)
    kca = to_heads(kc_a, NA_HEADS)
    vca = to_heads(vc_a, NA_HEADS)
    o_na = neighbourhood_attention(to_heads(q_a, NA_HEADS), to_heads(k_a, NA_HEADS), to_heads(v_a, NA_HEADS),
                                   kca, vca, rpb)
    y = merge_branches(o_na, o_ret, g_r, gate_na, gate_ret, w_na_out, w_ret_out, w_out)
    if not with_ctx_out:
        return y, None
    qc_a, qc_r, gc_r, gatec_na, gatec_ret = ctx_parts[4:]
    oc_na = context_attention(to_heads(qc_a, NA_HEADS), kca, vca)
    zeros = jnp.zeros_like(s_f)
    oc_ret = bidirectional_retention(to_heads(qc_r, RET_HEADS), kcr, vcr, log_gf, log_gb, zeros, zeros)
    y_ctx = merge_branches(oc_na, oc_ret, gc_r, gatec_na, gatec_ret, w_na_out, w_ret_out, w_out)
    return y, y_ctx


def setup_inputs(seed: int = 0) -> dict:
    key = jax.random.key(seed)
    ks = jax.random.split(key, 19)
    f32 = jnp.float32

    def nrm(k, shape, s):
        return s * jax.random.normal(k, shape, f32)

    col_scale = jnp.concatenate([jnp.full((s,), BETA if i in (1, 3) else 1.0, f32)
                                 for i, s in enumerate(COL_SPLITS)])
    base_logit = jnp.log(2.0 ** (5.0 + jnp.arange(RET_HEADS, dtype=f32)) - 1.0)
    return {
        "x": nrm(ks[0], (BATCH, SEQ, D_MODEL), 1.0),
        "c": nrm(ks[1], (BATCH, D_MODEL), 1.0),
        "ctx": nrm(ks[2], (BATCH, CTX_LEN, D_MODEL), 1.0),
        "c_ctx": nrm(ks[3], (D_MODEL,), 1.0),
        "w_ada": nrm(ks[4], (DEPTH, D_MODEL, N_MOD * D_MODEL), 0.5 * D_MODEL ** -0.5),
        "b_ada": nrm(ks[5], (DEPTH, N_MOD * D_MODEL), 0.02),
        "ln_g": 1.0 + nrm(ks[6], (DEPTH, 3, D_MODEL), 0.02),
        "ln_b": nrm(ks[7], (DEPTH, 3, D_MODEL), 0.02),
        "ffn1_w_in": nrm(ks[8], (DEPTH, D_MODEL, 2 * D_FF), D_MODEL ** -0.5),
        "ffn1_w_out": nrm(ks[9], (DEPTH, D_FF, D_MODEL), BETA * D_FF ** -0.5),
        "ffn2_w_in": nrm(ks[10], (DEPTH, D_MODEL, 2 * D_FF), D_MODEL ** -0.5),
        "ffn2_w_out": nrm(ks[11], (DEPTH, D_FF, D_MODEL), BETA * D_FF ** -0.5),
        "w_mix_in": nrm(ks[12], (DEPTH, D_MODEL, MIX_COLS), D_MODEL ** -0.5) * col_scale,
        "na_rpb": nrm(ks[13], (DEPTH, NA_HEADS, 2 * NA_WIN_ROWS - 1, 2 * NA_WIN_COLS - 1), 0.1),
        "ret_decay_fwd": base_logit + nrm(ks[14], (DEPTH, RET_HEADS), 0.05),
        "ret_decay_bwd": base_logit + nrm(ks[15], (DEPTH, RET_HEADS), 0.05),
        "w_na_out": nrm(ks[16], (DEPTH, NA_W, D_MODEL), BETA * NA_W ** -0.5),
        "w_ret_out": nrm(ks[17], (DEPTH, RET_V_W, D_MODEL), BETA * RET_V_W ** -0.5),
        "w_mix_out": nrm(ks[18], (DEPTH, D_MODEL, D_MODEL), BETA * D_MODEL ** -0.5),
    }


def reference(x, c, ctx, c_ctx, w_ada, b_ada, ln_g, ln_b, ffn1_w_in, ffn1_w_out,
              ffn2_w_in, ffn2_w_out, w_mix_in, na_rpb, ret_decay_fwd, ret_decay_bwd,
              w_na_out, w_ret_out, w_mix_out):
    batch = x.shape[0]
    s_c = jax.nn.silu(c)
    s_cc = jax.nn.silu(c_ctx)
    for l in range(DEPTH):
        last = l == DEPTH - 1
        m = (s_c @ w_ada[l] + b_ada[l]).reshape(batch, N_MOD, 1, D_MODEL)
        mc = (s_cc @ w_ada[l] + b_ada[l]).reshape(N_MOD, D_MODEL)
        x = layer_norm(ALPHA * x + m[:, 2] * (0.5 * swiglu(modulate(x, m[:, 0], m[:, 1]), ffn1_w_in[l], ffn1_w_out[l])),
                       ln_g[l, 0], ln_b[l, 0])
        ctx = layer_norm(ALPHA * ctx + mc[2] * (0.5 * swiglu(modulate(ctx, mc[0], mc[1]), ffn1_w_in[l], ffn1_w_out[l])),
                         ln_g[l, 0], ln_b[l, 0])
        y, y_ctx = token_mixers(modulate(x, m[:, 3], m[:, 4]), modulate(ctx, mc[3], mc[4]), w_mix_in[l], na_rpb[l],
                                ret_decay_fwd[l], ret_decay_bwd[l], w_na_out[l], w_ret_out[l], w_mix_out[l],
                                not last)
        x = layer_norm(ALPHA * x + m[:, 5] * y, ln_g[l, 1], ln_b[l, 1])
        x = layer_norm(ALPHA * x + m[:, 8] * (0.5 * swiglu(modulate(x, m[:, 6], m[:, 7]), ffn2_w_in[l], ffn2_w_out[l])),
                       ln_g[l, 2], ln_b[l, 2])
        if not last:
            ctx = layer_norm(ALPHA * ctx + mc[5] * y_ctx, ln_g[l, 1], ln_b[l, 1])
            ctx = layer_norm(ALPHA * ctx + mc[8] * (0.5 * swiglu(modulate(ctx, mc[6], mc[7]), ffn2_w_in[l], ffn2_w_out[l])),
                             ln_g[l, 2], ln_b[l, 2])
    return x
```

```python
import functools

import jax
import jax.numpy as jnp
from jax import lax
from jax.experimental import pallas as pl
from jax.experimental.pallas import tpu as pltpu

D_MODEL = 2048
BATCH = 2
SEQ = 8192
GRID_W = 64
CTX_LEN = 256
NA_HEADS = 16
NA_HEAD_DIM = 64
NA_W = NA_HEADS * NA_HEAD_DIM
NA_WIN_ROWS = 8
NA_WIN_COLS = 16
RET_HEADS = 8
RET_QK_DIM = 128
RET_V_DIM = 256
RET_QK_W = RET_HEADS * RET_QK_DIM
RET_V_W = RET_HEADS * RET_V_DIM
RET_CHUNK = 128
D_FF = 5632
N_MOD = 9
ROPE_BASE = 10000.0
LN_EPS = 1e-5
GN_EPS = 1e-6
NEG_INF = -1e30
DEPTH = 1
ALPHA = (2 * DEPTH) ** 0.25
MIX_COLS = 2 * NA_W + RET_QK_W + RET_V_W + NA_W + RET_QK_W + RET_V_W + 2 * D_MODEL
KV_COLS = 2 * NA_W + RET_QK_W + RET_V_W

OFF_KA = 0
OFF_VA = NA_W
OFF_KR = 2 * NA_W
OFF_VR = OFF_KR + RET_QK_W
OFF_QA = OFF_VR + RET_V_W
OFF_QR = OFF_QA + NA_W
OFF_GR = OFF_QR + RET_QK_W
OFF_GNA = OFF_GR + RET_V_W
OFF_GRET = OFF_GNA + D_MODEL

N_LAT = BATCH * SEQ
N_CTX = BATCH * CTX_LEN
GRID_ROWS = SEQ // GRID_W
N_CHUNKS = SEQ // RET_CHUNK

V7X_VMEM_BYTES = 64 * 1024 * 1024
VMEM_LIMIT_BYTES = V7X_VMEM_BYTES - 8 * 1024 * 1024
LANES = 128

FFN_ROWS = 512
FFN_COLS = 512
PROJ_ROWS = 512
PROJ_COLS = 1024
MERGE_ROWS = 256
ADA_COLS = 1024
MOD_ROWS = 8

BF16 = jnp.bfloat16
F32 = jnp.float32


def _dot(a, b):
    return jnp.dot(a, b, preferred_element_type=F32)


def _dot_nt(a, b):
    return lax.dot_general(a, b, (((1,), (1,)), ((), ())), preferred_element_type=F32)


def _dot_tn(a, b):
    return lax.dot_general(a, b, (((0,), (0,)), ((), ())), preferred_element_type=F32)


def _sigmoid(x):
    return 1.0 / (1.0 + jnp.exp(-x))


def _layer_norm(y, g, b, eps):
    mu = jnp.mean(y, axis=-1, keepdims=True)
    yc = y - mu
    var = jnp.mean(yc * yc, axis=-1, keepdims=True)
    return yc * lax.rsqrt(var + eps) * g + b


def _params(*semantics):
    return pltpu.CompilerParams(dimension_semantics=semantics, vmem_limit_bytes=VMEM_LIMIT_BYTES)


def _adaln_kernel(c_ref, w_ref, b_ref, o_ref):
    cc = c_ref[...]
    s = (cc * _sigmoid(cc)).astype(BF16)
    o_ref[...] = _dot(s, w_ref[...].astype(BF16)) + b_ref[...]


def _adaln(cc, w_ada, b_ada):
    n_out = N_MOD * D_MODEL
    return pl.pallas_call(
        _adaln_kernel,
        out_shape=jax.ShapeDtypeStruct((MOD_ROWS, n_out), F32),
        grid=(n_out // ADA_COLS,),
        in_specs=[
            pl.BlockSpec((MOD_ROWS, D_MODEL), lambda j: (0, 0)),
            pl.BlockSpec((D_MODEL, ADA_COLS), lambda j: (0, j)),
            pl.BlockSpec((1, ADA_COLS), lambda j: (0, j)),
        ],
        out_specs=pl.BlockSpec((MOD_ROWS, ADA_COLS), lambda j: (0, j)),
        compiler_params=_params("parallel"),
        name="adaln",
    )(cc, w_ada, b_ada)


def _ffn_kernel(x_ref, mod_ref, wg_ref, wu_ref, wo_ref, lng_ref, lnb_ref, o_ref, h_ref, acc_ref, *, k0):
    j = pl.program_id(1)

    @pl.when(j == 0)
    def _():
        shift = mod_ref[k0:k0 + 1, :]
        scale = mod_ref[k0 + 1:k0 + 2, :]
        h_ref[...] = (x_ref[...] * (1.0 + scale) + shift).astype(BF16)
        acc_ref[...] = jnp.zeros_like(acc_ref)

    h = h_ref[...]
    gate = _dot(h, wg_ref[...])
    up = _dot(h, wu_ref[...])
    act = (gate * _sigmoid(gate) * up).astype(BF16)
    acc_ref[...] += _dot(act, wo_ref[...])

    @pl.when(j == pl.num_programs(1) - 1)
    def _():
        g = mod_ref[k0 + 2:k0 + 3, :]
        y = ALPHA * x_ref[...] + g * (0.5 * acc_ref[...])
        o_ref[...] = _layer_norm(y, lng_ref[...], lnb_ref[...], LN_EPS)


def _ffn(x, mod, w_in, w_out, ln_g, ln_b, *, k0, rows_per_mod, mod_base):
    n_rows = x.shape[0]
    tm, tf = FFN_ROWS, FFN_COLS
    tiles_per_mod = rows_per_mod // tm
    n_col = D_FF // tf
    return pl.pallas_call(
        functools.partial(_ffn_kernel, k0=k0),
        out_shape=jax.ShapeDtypeStruct((n_rows, D_MODEL), F32),
        grid=(n_rows // tm, n_col),
        in_specs=[
            pl.BlockSpec((tm, D_MODEL), lambda i, j: (i, 0)),
            pl.BlockSpec((None, N_MOD, D_MODEL), lambda i, j: (mod_base + i // tiles_per_mod, 0, 0)),
            pl.BlockSpec((D_MODEL, tf), lambda i, j: (0, j)),
            pl.BlockSpec((D_MODEL, tf), lambda i, j: (0, j + n_col)),
            pl.BlockSpec((tf, D_MODEL), lambda i, j: (j, 0)),
            pl.BlockSpec((1, D_MODEL), lambda i, j: (0, 0)),
            pl.BlockSpec((1, D_MODEL), lambda i, j: (0, 0)),
        ],
        out_specs=pl.BlockSpec((tm, D_MODEL), lambda i, j: (i, 0)),
        scratch_shapes=[pltpu.VMEM((tm, D_MODEL), BF16), pltpu.VMEM((tm, D_MODEL), F32)],
        compiler_params=_params("parallel", "arbitrary"),
        name="ffn",
    )(x, mod, w_in, w_in, w_out, ln_g, ln_b)


def _swap_pairs(x, lane_is_first):
    return jnp.where(lane_is_first, pltpu.roll(x, LANES - 32, 1), pltpu.roll(x, 32, 1))


def _rope_heads(acc, cos, sin):
    lane = lax.broadcasted_iota(jnp.int32, (1, LANES), 1)
    first = (lane & 32) == 0
    outs = []
    for h in range(acc.shape[1] // LANES):
        xh = acc[:, h * LANES:(h + 1) * LANES]
        outs.append(xh * cos + _swap_pairs(xh, first) * sin)
    return jnp.concatenate(outs, axis=1)


def _proj_kernel(x_ref, mod_ref, w_ref, cos_ref, sin_ref, o_ref, h_ref, *, rope):
    j = pl.program_id(1)
    c = PROJ_COLS

    @pl.when(j == 0)
    def _():
        shift = mod_ref[3:4, :]
        scale = mod_ref[4:5, :]
        h_ref[...] = (x_ref[...] * (1.0 + scale) + shift).astype(BF16)

    acc = _dot(h_ref[...], w_ref[...])

    def is_in(lo, hi):
        return (j >= lo // c) & (j < hi // c)

    @pl.when(is_in(OFF_KA, OFF_KR) | is_in(OFF_VR, OFF_QA))
    def _():
        o_ref[...] = acc.astype(BF16)

    @pl.when(is_in(OFF_KR, OFF_VR))
    def _():
        k = _rope_heads(acc, cos_ref[...], sin_ref[...]) if rope else acc
        o_ref[...] = (k * (RET_QK_DIM ** -0.5)).astype(BF16)

    @pl.when(is_in(OFF_QA, OFF_QR))
    def _():
        o_ref[...] = (acc * (NA_HEAD_DIM ** -0.5)).astype(BF16)

    @pl.when(is_in(OFF_QR, OFF_GR))
    def _():
        q = _rope_heads(acc, cos_ref[...], sin_ref[...]) if rope else acc
        o_ref[...] = q.astype(BF16)

    @pl.when(is_in(OFF_GR, OFF_GNA))
    def _():
        o_ref[...] = (acc * _sigmoid(acc)).astype(BF16)

    @pl.when(is_in(OFF_GNA, MIX_COLS))
    def _():
        o_ref[...] = _sigmoid(acc).astype(BF16)


def _proj(x, mod, w, cos, sin, *, n_cols, rows_per_mod, mod_base, rope):
    n_rows = x.shape[0]
    tm, tn = PROJ_ROWS, PROJ_COLS
    tiles_per_mod = rows_per_mod // tm
    pos_tiles = cos.shape[0] // tm
    return pl.pallas_call(
        functools.partial(_proj_kernel, rope=rope),
        out_shape=jax.ShapeDtypeStruct((n_rows, n_cols), BF16),
        grid=(n_rows // tm, n_cols // tn),
        in_specs=[
            pl.BlockSpec((tm, D_MODEL), lambda i, j: (i, 0)),
            pl.BlockSpec((None, N_MOD, D_MODEL), lambda i, j: (mod_base + i // tiles_per_mod, 0, 0)),
            pl.BlockSpec((D_MODEL, tn), lambda i, j: (0, j)),
            pl.BlockSpec((tm, LANES), lambda i, j: (i % pos_tiles, 0)),
            pl.BlockSpec((tm, LANES), lambda i, j: (i % pos_tiles, 0)),
        ],
        out_specs=pl.BlockSpec((tm, tn), lambda i, j: (i, j)),
        scratch_shapes=[pltpu.VMEM((tm, D_MODEL), BF16)],
        compiler_params=_params("parallel", "arbitrary"),
        name="mix_proj",
    )(x, mod, w, cos, sin)


def _na_kernel(q_ref, k_ref, v_ref, kc_ref, vc_ref, bias_ref, o_ref):
    lane = lax.broadcasted_iota(jnp.int32, (1, LANES), 1)
    head_lanes = (lane < NA_HEAD_DIM, lane >= NA_HEAD_DIM)
    zero = jnp.zeros((), BF16)
    kc = kc_ref[...]
    vc = vc_ref[...]
    kc_h = [jnp.where(m, kc, zero) for m in head_lanes]
    vc_h = [jnp.where(m, vc, zero) for m in head_lanes]
    win = NA_WIN_ROWS * GRID_W

    def row(r, carry):
        rs = jnp.clip(r - NA_WIN_ROWS // 2, 0, GRID_ROWS - NA_WIN_ROWS)
        delta = r - rs
        q = q_ref[pl.ds(pl.multiple_of(r * GRID_W, GRID_W), GRID_W), :]
        kw = k_ref[pl.ds(pl.multiple_of(rs * GRID_W, GRID_W), win), :]
        vw = v_ref[pl.ds(pl.multiple_of(rs * GRID_W, GRID_W), win), :]
        out = jnp.zeros((GRID_W, LANES), F32)
        for hh in range(2):
            kh = jnp.where(head_lanes[hh], kw, zero)
            vh = jnp.where(head_lanes[hh], vw, zero)
            s_lat = _dot_nt(q, kh) + bias_ref[hh, delta]
            s_ctx = _dot_nt(q, kc_h[hh])
            mx = jnp.maximum(jnp.max(s_lat, axis=-1, keepdims=True), jnp.max(s_ctx, axis=-1, keepdims=True))
            p_lat = jnp.exp(s_lat - mx)
            p_ctx = jnp.exp(s_ctx - mx)
            den = jnp.sum(p_lat, axis=-1, keepdims=True) + jnp.sum(p_ctx, axis=-1, keepdims=True)
            o_h = _dot(p_lat.astype(BF16), vh) + _dot(p_ctx.astype(BF16), vc_h[hh])
            out = out + o_h / den
        o_ref[pl.ds(pl.multiple_of(r * GRID_W, GRID_W), GRID_W), :] = out.astype(BF16)
        return carry

    lax.fori_loop(0, GRID_ROWS, row, 0)


def _na(proj, cproj, bias):
    n_pairs = NA_HEADS // 2
    blk = lambda off: off // LANES
    return pl.pallas_call(
        _na_kernel,
        out_shape=jax.ShapeDtypeStruct((N_LAT, NA_W), BF16),
        grid=(BATCH, n_pairs),
        in_specs=[
            pl.BlockSpec((SEQ, LANES), lambda b, p: (b, blk(OFF_QA) + p)),
            pl.BlockSpec((SEQ, LANES), lambda b, p: (b, blk(OFF_KA) + p)),
            pl.BlockSpec((SEQ, LANES), lambda b, p: (b, blk(OFF_VA) + p)),
            pl.BlockSpec((CTX_LEN, LANES), lambda b, p: (b, blk(OFF_KA) + p)),
            pl.BlockSpec((CTX_LEN, LANES), lambda b, p: (b, blk(OFF_VA) + p)),
            pl.BlockSpec((2, NA_WIN_ROWS, GRID_W, NA_WIN_ROWS * GRID_W), lambda b, p: (p, 0, 0, 0)),
        ],
        out_specs=pl.BlockSpec((SEQ, LANES), lambda b, p: (b, p)),
        compiler_params=_params("parallel", "parallel"),
        name="nbr_attn",
    )(proj, proj, proj, cproj, cproj, bias)


def _ret_kernel(lg_ref, q_ref, k_ref, v_ref, g_ref, kc_ref, vc_ref, o_ref, sb_ref):
    h = pl.program_id(1)
    lgf = lg_ref[0, h]
    lgb = lg_ref[1, h]
    c = RET_CHUNK
    ri = lax.broadcasted_iota(jnp.int32, (c, c), 0).astype(F32)
    ci = lax.broadcasted_iota(jnp.int32, (c, c), 1).astype(F32)
    diff = ri - ci
    decay = (jnp.where(diff >= 0, jnp.exp(lgf * jnp.maximum(diff, 0.0)), 0.0)
             + jnp.where(diff < 0, jnp.exp(lgb * jnp.maximum(-diff, 0.0)), 0.0))
    zeta_f = jnp.exp(lgf * (c - 1 - ri))
    zeta_b = jnp.exp(lgb * ri)
    rv = lax.broadcasted_iota(jnp.int32, (c, RET_V_DIM), 0).astype(F32)
    xi_f = jnp.exp(lgf * (rv + 1.0))
    xi_b = jnp.exp(lgb * (c - rv))
    g_f = jnp.exp(jnp.full((c, RET_V_DIM), lgf * c, F32))
    g_b = jnp.exp(jnp.full((c, RET_V_DIM), lgb * c, F32))

    pc = lax.broadcasted_iota(jnp.int32, (CTX_LEN, RET_QK_DIM), 0).astype(F32)
    kc = kc_ref[...].astype(F32)
    vc = vc_ref[...]
    s0_f = _dot_tn((kc * jnp.exp(lgf * (CTX_LEN - 1 - pc))).astype(BF16), vc)
    s0_b = _dot_tn((kc * jnp.exp(lgb * pc)).astype(BF16), vc)

    def chunk(n):
        return pl.ds(pl.multiple_of(n * c, c), c)

    def bwd(t, s):
        n = N_CHUNKS - 1 - t
        sb_ref[n] = s.astype(BF16)
        kz = (k_ref[chunk(n), :].astype(F32) * zeta_b).astype(BF16)
        return g_b * s + _dot_tn(kz, v_ref[chunk(n), :])

    lax.fori_loop(0, N_CHUNKS, bwd, s0_b)

    def fwd(n, s):
        q = q_ref[chunk(n), :]
        k = k_ref[chunk(n), :]
        v = v_ref[chunk(n), :]
        scores = (_dot_nt(q, k) * decay).astype(BF16)
        o = (_dot(scores, v)
             + _dot(q, s.astype(BF16)) * xi_f
             + _dot(q, sb_ref[n]) * xi_b)
        mu = jnp.mean(o, axis=-1, keepdims=True)
        oc = o - mu
        var = jnp.mean(oc * oc, axis=-1, keepdims=True)
        gn = oc * lax.rsqrt(var + GN_EPS)
        o_ref[chunk(n), :] = (g_ref[chunk(n), :].astype(F32) * gn).astype(BF16)
        kz = (k.astype(F32) * zeta_f).astype(BF16)
        return g_f * s + _dot_tn(kz, v)

    lax.fori_loop(0, N_CHUNKS, fwd, s0_f)


def _ret(log_gamma, proj, cproj):
    qk = lambda off: off // RET_QK_DIM
    vv = lambda off: off // RET_V_DIM
    return pl.pallas_call(
        _ret_kernel,
        out_shape=jax.ShapeDtypeStruct((N_LAT, RET_V_W), BF16),
        grid=(BATCH, RET_HEADS),
        in_specs=[
            pl.BlockSpec(memory_space=pltpu.SMEM),
            pl.BlockSpec((SEQ, RET_QK_DIM), lambda b, h: (b, qk(OFF_QR) + h)),
            pl.BlockSpec((SEQ, RET_QK_DIM), lambda b, h: (b, qk(OFF_KR) + h)),
            pl.BlockSpec((SEQ, RET_V_DIM), lambda b, h: (b, vv(OFF_VR) + h)),
            pl.BlockSpec((SEQ, RET_V_DIM), lambda b, h: (b, vv(OFF_GR) + h)),
            pl.BlockSpec((CTX_LEN, RET_QK_DIM), lambda b, h: (b, qk(OFF_KR) + h)),
            pl.BlockSpec((CTX_LEN, RET_V_DIM), lambda b, h: (b, vv(OFF_VR) + h)),
        ],
        out_specs=pl.BlockSpec((SEQ, RET_V_DIM), lambda b, h: (b, h)),
        scratch_shapes=[pltpu.VMEM((N_CHUNKS, RET_QK_DIM, RET_V_DIM), BF16)],
        compiler_params=_params("parallel", "parallel"),
        name="retention",
    )(log_gamma, proj, proj, proj, proj, cproj, cproj)


def _merge_kernel(ona_ref, ret_ref, gna0_ref, gna1_ref, gr0_ref, gr1_ref, x_ref, mod_ref,
                  wna_ref, wret_ref, wout_ref, lng_ref, lnb_ref, o_ref):
    y_na = _dot(ona_ref[...], wna_ref[...])
    y_ret = _dot(ret_ref[...], wret_ref[...])
    gate_na = jnp.concatenate([gna0_ref[...], gna1_ref[...]], axis=1).astype(F32)
    gate_ret = jnp.concatenate([gr0_ref[...], gr1_ref[...]], axis=1).astype(F32)
    z = (gate_na * y_na + gate_ret * y_ret).astype(BF16)
    y = _dot(z, wout_ref[...])
    g = mod_ref[5:6, :]
    o_ref[...] = _layer_norm(ALPHA * x_ref[...] + g * y, lng_ref[...], lnb_ref[...], LN_EPS)


def _merge(o_na, ret, proj, x, mod, w_na, w_ret, w_out, ln_g, ln_b):
    tm = MERGE_ROWS
    tiles_per_batch = SEQ // tm
    half = D_MODEL // 2
    gate = lambda off: pl.BlockSpec((tm, half), lambda i: (i, off // half))
    resident = lambda shape: pl.BlockSpec(shape, lambda i: (0, 0), pipeline_mode=pl.Buffered(1))
    return pl.pallas_call(
        _merge_kernel,
        out_shape=jax.ShapeDtypeStruct((N_LAT, D_MODEL), F32),
        grid=(N_LAT // tm,),
        in_specs=[
            pl.BlockSpec((tm, NA_W), lambda i: (i, 0)),
            pl.BlockSpec((tm, RET_V_W), lambda i: (i, 0)),
            gate(OFF_GNA), gate(OFF_GNA + half), gate(OFF_GRET), gate(OFF_GRET + half),
            pl.BlockSpec((tm, D_MODEL), lambda i: (i, 0)),
            pl.BlockSpec((None, N_MOD, D_MODEL), lambda i: (i // tiles_per_batch, 0, 0)),
            resident((NA_W, D_MODEL)),
            resident((RET_V_W, D_MODEL)),
            resident((D_MODEL, D_MODEL)),
            pl.BlockSpec((1, D_MODEL), lambda i: (0, 0)),
            pl.BlockSpec((1, D_MODEL), lambda i: (0, 0)),
        ],
        out_specs=pl.BlockSpec((tm, D_MODEL), lambda i: (i, 0)),
        compiler_params=_params("parallel"),
        name="merge",
    )(o_na, ret, proj, proj, proj, proj, x, mod, w_na, w_ret, w_out, ln_g, ln_b)


def _rope_tables():
    nfreq = RET_QK_DIM // 4
    pos = jnp.arange(SEQ)
    prow = (pos // GRID_W).astype(F32)
    pcol = (pos % GRID_W).astype(F32)
    freqs = ROPE_BASE ** (-jnp.arange(nfreq, dtype=F32) / nfreq)
    ang = jnp.stack([prow, pcol], axis=-1)[:, :, None] * freqs
    cos = jnp.cos(ang)
    sin = jnp.sin(ang)
    cos_t = jnp.stack([cos, cos], axis=2).reshape(SEQ, RET_QK_DIM)
    sin_t = jnp.stack([-sin, sin], axis=2).reshape(SEQ, RET_QK_DIM)
    return cos_t, sin_t


def _na_bias_table(rpb):
    wc = NA_WIN_COLS
    col = jnp.arange(GRID_W)
    col_start = jnp.clip(col - wc // 2, 0, GRID_W - wc)
    col_mask = (col[None, :] >= col_start[:, None]) & (col[None, :] < col_start[:, None] + wc)
    dc_idx = jnp.clip(col[None, :] - col[:, None] + wc - 1, 0, 2 * wc - 2)
    t = jnp.where(col_mask[None, None], rpb[:, :, dc_idx], NEG_INF)
    i = jnp.arange(NA_WIN_ROWS)
    delta = jnp.arange(NA_WIN_ROWS)
    dr = i[None, :] - delta[:, None] + NA_WIN_ROWS - 1
    tb = t[:, dr]
    return tb.transpose(0, 1, 3, 2, 4).reshape(NA_HEADS, NA_WIN_ROWS, GRID_W, NA_WIN_ROWS * GRID_W)


def kernel(x, c, ctx, c_ctx, w_ada, b_ada, ln_g, ln_b, ffn1_w_in, ffn1_w_out, ffn2_w_in, ffn2_w_out,
           w_mix_in, na_rpb, ret_decay_fwd, ret_decay_bwd, w_na_out, w_ret_out, w_mix_out):
    assert x.shape == (BATCH, SEQ, D_MODEL) and ctx.shape == (BATCH, CTX_LEN, D_MODEL)
    assert w_ada.shape == (DEPTH, D_MODEL, N_MOD * D_MODEL) and w_mix_in.shape == (DEPTH, D_MODEL, MIX_COLS)
    l = 0
    ctx_mod = BATCH
    cc = jnp.zeros((MOD_ROWS, D_MODEL), F32).at[:BATCH].set(c).at[ctx_mod].set(c_ctx)
    mod = _adaln(cc, w_ada[l], b_ada[l].reshape(1, -1)).reshape(MOD_ROWS, N_MOD, D_MODEL)

    row = lambda v: v.reshape(1, D_MODEL)
    w1_in, w1_out = ffn1_w_in[l].astype(BF16), ffn1_w_out[l].astype(BF16)
    w2_in, w2_out = ffn2_w_in[l].astype(BF16), ffn2_w_out[l].astype(BF16)
    w_mix = w_mix_in[l].astype(BF16)
    w_na, w_ret, w_out = w_na_out[l].astype(BF16), w_ret_out[l].astype(BF16), w_mix_out[l].astype(BF16)

    xf = x.reshape(N_LAT, D_MODEL)
    cf = ctx.reshape(N_CTX, D_MODEL)
    x1 = _ffn(xf, mod, w1_in, w1_out, row(ln_g[l, 0]), row(ln_b[l, 0]), k0=0, rows_per_mod=SEQ, mod_base=0)
    c1 = _ffn(cf, mod, w1_in, w1_out, row(ln_g[l, 0]), row(ln_b[l, 0]), k0=0, rows_per_mod=N_CTX, mod_base=ctx_mod)

    cos_t, sin_t = _rope_tables()
    proj = _proj(x1, mod, w_mix, cos_t, sin_t, n_cols=MIX_COLS, rows_per_mod=SEQ, mod_base=0, rope=True)
    cproj = _proj(c1, mod, w_mix, cos_t, sin_t, n_cols=KV_COLS, rows_per_mod=N_CTX, mod_base=ctx_mod, rope=False)

    o_na = _na(proj, cproj, _na_bias_table(na_rpb[l]))
    log_gamma = jnp.stack([jax.nn.log_sigmoid(ret_decay_fwd[l].astype(F32)),
                           jax.nn.log_sigmoid(ret_decay_bwd[l].astype(F32))])
    ret = _ret(log_gamma, proj, cproj)

    x2 = _merge(o_na, ret, proj, x1, mod, w_na, w_ret, w_out, row(ln_g[l, 1]), row(ln_b[l, 1]))
    x3 = _ffn(x2, mod, w2_in, w2_out, row(ln_g[l, 2]), row(ln_b[l, 2]), k0=6, rows_per_mod=SEQ, mod_base=0)
    return x3.reshape(BATCH, SEQ, D_MODEL)
```

```python
import functools

import jax
import jax.numpy as jnp
from jax import lax
from jax.experimental import pallas as pl
from jax.experimental.pallas import tpu as pltpu

D_MODEL = 2048
BATCH = 2
SEQ = 8192
GRID_W = 64
CTX_LEN = 256
NA_HEADS = 16
NA_HEAD_DIM = 64
NA_W = NA_HEADS * NA_HEAD_DIM
NA_WIN_ROWS = 8
NA_WIN_COLS = 16
RET_HEADS = 8
RET_QK_DIM = 128
RET_V_DIM = 256
RET_QK_W = RET_HEADS * RET_QK_DIM
RET_V_W = RET_HEADS * RET_V_DIM
RET_CHUNK = 128
D_FF = 5632
N_MOD = 9
ROPE_BASE = 10000.0
LN_EPS = 1e-5
GN_EPS = 1e-6
NEG_INF = -1e30
DEPTH = 1
ALPHA = (2 * DEPTH) ** 0.25
MIX_COLS = 2 * NA_W + RET_QK_W + RET_V_W + NA_W + RET_QK_W + RET_V_W + 2 * D_MODEL
KV_COLS = 2 * NA_W + RET_QK_W + RET_V_W

OFF_KA = 0
OFF_VA = NA_W
OFF_KR = 2 * NA_W
OFF_VR = OFF_KR + RET_QK_W
OFF_QA = OFF_VR + RET_V_W
OFF_QR = OFF_QA + NA_W
OFF_GR = OFF_QR + RET_QK_W
OFF_GNA = OFF_GR + RET_V_W
OFF_GRET = OFF_GNA + D_MODEL

ROPE_COLS = 2 * RET_QK_W
ROPE_KR = 0
ROPE_QR = RET_QK_W
REST_KA = 0
REST_VA = REST_KA + NA_W
REST_VR = REST_VA + NA_W
REST_QA = REST_VR + RET_V_W
REST_GR = REST_QA + NA_W
REST_GNA = REST_GR + RET_V_W
REST_GRET = REST_GNA + D_MODEL
REST_COLS = REST_GRET + D_MODEL

KIND_SCALE = 0
KIND_SILU = 1
KIND_SIGMOID = 2

N_LAT = BATCH * SEQ
N_CTX = BATCH * CTX_LEN
GRID_ROWS = SEQ // GRID_W
N_CHUNKS = SEQ // RET_CHUNK

V7X_VMEM_BYTES = 64 * 1024 * 1024
VMEM_LIMIT_BYTES = V7X_VMEM_BYTES - 8 * 1024 * 1024
LANES = 128

FFN_ROWS = 512
FFN_COLS = 512
PROJ_ROWS = 1024
PROJ_COLS = 1408
CTX_PROJ_COLS = 2560
MERGE_ROWS = 256
ADA_COLS = 1024
MOD_ROWS = 8
NA_UNROLL = 4
RET_UNROLL = 4

BF16 = jnp.bfloat16
F32 = jnp.float32


def _dot(a, b):
    return jnp.dot(a, b, preferred_element_type=F32)


def _dot_nt(a, b):
    return lax.dot_general(a, b, (((1,), (1,)), ((), ())), preferred_element_type=F32)


def _dot_tn(a, b):
    return lax.dot_general(a, b, (((0,), (0,)), ((), ())), preferred_element_type=F32)


def _sigmoid(x):
    return 1.0 / (1.0 + jnp.exp(-x))


def _layer_norm(y, g, b, eps):
    mu = jnp.mean(y, axis=-1, keepdims=True)
    yc = y - mu
    var = jnp.mean(yc * yc, axis=-1, keepdims=True)
    return yc * lax.rsqrt(var + eps) * g + b


def _params(*semantics):
    return pltpu.CompilerParams(dimension_semantics=semantics, vmem_limit_bytes=VMEM_LIMIT_BYTES)


def _adaln_kernel(c_ref, w_ref, b_ref, o_ref):
    cc = c_ref[...]
    s = (cc * _sigmoid(cc)).astype(BF16)
    o_ref[...] = _dot(s, w_ref[...].astype(BF16)) + b_ref[...]


def _adaln(cc, w_ada, b_ada):
    n_out = N_MOD * D_MODEL
    return pl.pallas_call(
        _adaln_kernel,
        out_shape=jax.ShapeDtypeStruct((MOD_ROWS, n_out), F32),
        grid=(n_out // ADA_COLS,),
        in_specs=[
            pl.BlockSpec((MOD_ROWS, D_MODEL), lambda j: (0, 0)),
            pl.BlockSpec((D_MODEL, ADA_COLS), lambda j: (0, j)),
            pl.BlockSpec((1, ADA_COLS), lambda j: (0, j)),
        ],
        out_specs=pl.BlockSpec((MOD_ROWS, ADA_COLS), lambda j: (0, j)),
        compiler_params=_params("parallel"),
        name="adaln",
    )(cc, w_ada, b_ada)


def _ffn_kernel(x_ref, mod_ref, wg_ref, wu_ref, wo_ref, lng_ref, lnb_ref, o_ref, h_ref, acc_ref, *, k0):
    j = pl.program_id(1)

    @pl.when(j == 0)
    def _():
        shift = mod_ref[k0:k0 + 1, :]
        scale = mod_ref[k0 + 1:k0 + 2, :]
        h_ref[...] = (x_ref[...] * (1.0 + scale) + shift).astype(BF16)
        acc_ref[...] = jnp.zeros_like(acc_ref)

    h = h_ref[...]
    gate = _dot(h, wg_ref[...])
    up = _dot(h, wu_ref[...])
    act = (gate * _sigmoid(gate) * up).astype(BF16)
    acc_ref[...] += _dot(act, wo_ref[...])

    @pl.when(j == pl.num_programs(1) - 1)
    def _():
        g = mod_ref[k0 + 2:k0 + 3, :]
        y = ALPHA * x_ref[...] + g * (0.5 * acc_ref[...])
        o_ref[...] = _layer_norm(y, lng_ref[...], lnb_ref[...], LN_EPS)


def _ffn(x, mod, w_in, w_out, ln_g, ln_b, *, k0, rows_per_mod, mod_base):
    n_rows = x.shape[0]
    tm, tf = FFN_ROWS, FFN_COLS
    tiles_per_mod = rows_per_mod // tm
    n_col = D_FF // tf
    return pl.pallas_call(
        functools.partial(_ffn_kernel, k0=k0),
        out_shape=jax.ShapeDtypeStruct((n_rows, D_MODEL), F32),
        grid=(n_rows // tm, n_col),
        in_specs=[
            pl.BlockSpec((tm, D_MODEL), lambda i, j: (i, 0)),
            pl.BlockSpec((None, N_MOD, D_MODEL), lambda i, j: (mod_base + i // tiles_per_mod, 0, 0)),
            pl.BlockSpec((D_MODEL, tf), lambda i, j: (0, j)),
            pl.BlockSpec((D_MODEL, tf), lambda i, j: (0, j + n_col)),
            pl.BlockSpec((tf, D_MODEL), lambda i, j: (j, 0)),
            pl.BlockSpec((1, D_MODEL), lambda i, j: (0, 0)),
            pl.BlockSpec((1, D_MODEL), lambda i, j: (0, 0)),
        ],
        out_specs=pl.BlockSpec((tm, D_MODEL), lambda i, j: (i, 0)),
        scratch_shapes=[pltpu.VMEM((tm, D_MODEL), BF16), pltpu.VMEM((tm, D_MODEL), F32)],
        compiler_params=_params("parallel", "arbitrary"),
        name="ffn",
    )(x, mod, w_in, w_in, w_out, ln_g, ln_b)


def _swap_pairs(x, lane_is_first):
    return jnp.where(lane_is_first, pltpu.roll(x, LANES - 32, 1), pltpu.roll(x, 32, 1))


def _mix_modulate(x_ref, mod_ref):
    shift = mod_ref[3:4, :]
    scale = mod_ref[4:5, :]
    return (x_ref[...] * (1.0 + scale) + shift).astype(BF16)


def _proj_kernel(x_ref, mod_ref, w_ref, kind_ref, scale_ref, o_ref, h_ref, *, gated):
    @pl.when(pl.program_id(1) == 0)
    def _():
        h_ref[...] = _mix_modulate(x_ref, mod_ref)

    acc = _dot(h_ref[...], w_ref[...])
    out = acc * scale_ref[...]
    if gated:
        kind = kind_ref[...]
        gate = _sigmoid(acc) * jnp.where(kind == KIND_SILU, acc, 1.0)
        out = jnp.where(kind == KIND_SCALE, out, gate)
    o_ref[...] = out.astype(BF16)


def _proj(x, mod, w, kind, scale, *, n_cols, tm, tn, rows_per_mod, mod_base, gated):
    n_rows = x.shape[0]
    tiles_per_mod = rows_per_mod // tm
    return pl.pallas_call(
        functools.partial(_proj_kernel, gated=gated),
        out_shape=jax.ShapeDtypeStruct((n_rows, n_cols), BF16),
        grid=(n_rows // tm, n_cols // tn),
        in_specs=[
            pl.BlockSpec((tm, D_MODEL), lambda i, j: (i, 0)),
            pl.BlockSpec((None, N_MOD, D_MODEL), lambda i, j: (mod_base + i // tiles_per_mod, 0, 0)),
            pl.BlockSpec((D_MODEL, tn), lambda i, j: (0, j)),
            pl.BlockSpec((1, tn), lambda i, j: (0, j)),
            pl.BlockSpec((1, tn), lambda i, j: (0, j)),
        ],
        out_specs=pl.BlockSpec((tm, tn), lambda i, j: (i, j)),
        scratch_shapes=[pltpu.VMEM((tm, D_MODEL), BF16)],
        compiler_params=_params("parallel", "arbitrary"),
        name="mix_proj",
    )(x, mod, w, kind, scale)


def _rope_proj_kernel(x_ref, mod_ref, w_ref, cos_ref, sin_ref, o_ref):
    acc = _dot(_mix_modulate(x_ref, mod_ref), w_ref[...])
    cos = cos_ref[...]
    sin = sin_ref[...]
    lane = lax.broadcasted_iota(jnp.int32, (1, LANES), 1)
    first = (lane & 32) == 0
    for h in range(ROPE_COLS // LANES):
        cols = slice(h * LANES, (h + 1) * LANES)
        xh = acc[:, cols]
        r = xh * cos + _swap_pairs(xh, first) * sin
        if h * LANES < ROPE_QR:
            r = r * (RET_QK_DIM ** -0.5)
        o_ref[:, cols] = r.astype(BF16)


def _rope_proj(x, mod, w, cos, sin):
    tm = PROJ_ROWS
    tiles_per_batch = SEQ // tm
    return pl.pallas_call(
        _rope_proj_kernel,
        out_shape=jax.ShapeDtypeStruct((N_LAT, ROPE_COLS), BF16),
        grid=(N_LAT // tm,),
        in_specs=[
            pl.BlockSpec((tm, D_MODEL), lambda i: (i, 0)),
            pl.BlockSpec((None, N_MOD, D_MODEL), lambda i: (i // tiles_per_batch, 0, 0)),
            pl.BlockSpec((D_MODEL, ROPE_COLS), lambda i: (0, 0), pipeline_mode=pl.Buffered(1)),
            pl.BlockSpec((tm, LANES), lambda i: (i % tiles_per_batch, 0)),
            pl.BlockSpec((tm, LANES), lambda i: (i % tiles_per_batch, 0)),
        ],
        out_specs=pl.BlockSpec((tm, ROPE_COLS), lambda i: (i, 0)),
        compiler_params=_params("parallel"),
        name="rope_proj",
    )(x, mod, w, cos, sin)


def _na_kernel(q_ref, k_ref, v_ref, kc_ref, vc_ref, bias_ref, o_ref):
    lane = lax.broadcasted_iota(jnp.int32, (1, LANES), 1)
    first_head = lane < NA_HEAD_DIM
    zero = jnp.zeros((), BF16)
    kc = kc_ref[...]
    vc = vc_ref[...]
    win = NA_WIN_ROWS * GRID_W

    def one_row(r):
        rs = jnp.clip(r - NA_WIN_ROWS // 2, 0, GRID_ROWS - NA_WIN_ROWS)
        delta = r - rs
        q = q_ref[pl.ds(pl.multiple_of(r * GRID_W, GRID_W), GRID_W), :]
        kw = k_ref[pl.ds(pl.multiple_of(rs * GRID_W, GRID_W), win), :]
        vw = v_ref[pl.ds(pl.multiple_of(rs * GRID_W, GRID_W), win), :]
        q2 = jnp.concatenate([jnp.where(first_head, q, zero), jnp.where(first_head, zero, q)], axis=0)
        s_lat = _dot_nt(q2, kw) + bias_ref[delta]
        s_ctx = _dot_nt(q2, kc)
        mx = jnp.maximum(jnp.max(s_lat, axis=-1, keepdims=True), jnp.max(s_ctx, axis=-1, keepdims=True))
        p_lat = jnp.exp(s_lat - mx)
        p_ctx = jnp.exp(s_ctx - mx)
        den = jnp.sum(p_lat, axis=-1, keepdims=True) + jnp.sum(p_ctx, axis=-1, keepdims=True)
        o2 = (_dot(p_lat.astype(BF16), vw) + _dot(p_ctx.astype(BF16), vc)) / den
        out = jnp.where(first_head, o2[:GRID_W], o2[GRID_W:])
        o_ref[pl.ds(pl.multiple_of(r * GRID_W, GRID_W), GRID_W), :] = out.astype(BF16)

    def rows(i, carry):
        for u in range(NA_UNROLL):
            one_row(i * NA_UNROLL + u)
        return carry

    lax.fori_loop(0, GRID_ROWS // NA_UNROLL, rows, 0)


def _na(proj, cproj, bias):
    n_pairs = NA_HEADS // 2
    blk = lambda off: off // LANES
    return pl.pallas_call(
        _na_kernel,
        out_shape=jax.ShapeDtypeStruct((N_LAT, NA_W), BF16),
        grid=(BATCH, n_pairs),
        in_specs=[
            pl.BlockSpec((SEQ, LANES), lambda b, p: (b, blk(REST_QA) + p)),
            pl.BlockSpec((SEQ, LANES), lambda b, p: (b, blk(REST_KA) + p)),
            pl.BlockSpec((SEQ, LANES), lambda b, p: (b, blk(REST_VA) + p)),
            pl.BlockSpec((CTX_LEN, LANES), lambda b, p: (b, blk(OFF_KA) + p)),
            pl.BlockSpec((CTX_LEN, LANES), lambda b, p: (b, blk(OFF_VA) + p)),
            pl.BlockSpec((None, NA_WIN_ROWS, 2 * GRID_W, NA_WIN_ROWS * GRID_W), lambda b, p: (p, 0, 0, 0)),
        ],
        out_specs=pl.BlockSpec((SEQ, LANES), lambda b, p: (b, p)),
        compiler_params=_params("parallel", "parallel"),
        name="nbr_attn",
    )(proj, proj, proj, cproj, cproj, bias)


def _ret_kernel(lg_ref, q_ref, k_ref, v_ref, g_ref, kc_ref, vc_ref, o_ref, sb_ref):
    h = pl.program_id(1)
    lgf = lg_ref[0, h]
    lgb = lg_ref[1, h]
    c = RET_CHUNK
    ri = lax.broadcasted_iota(jnp.int32, (c, c), 0).astype(F32)
    ci = lax.broadcasted_iota(jnp.int32, (c, c), 1).astype(F32)
    diff = ri - ci
    decay = (jnp.where(diff >= 0, jnp.exp(lgf * jnp.maximum(diff, 0.0)), 0.0)
             + jnp.where(diff < 0, jnp.exp(lgb * jnp.maximum(-diff, 0.0)), 0.0))
    zeta_f = jnp.exp(lgf * (c - 1 - ri))
    zeta_b = jnp.exp(lgb * ri)
    rv = lax.broadcasted_iota(jnp.int32, (c, RET_V_DIM), 0).astype(F32)
    xi_f = jnp.exp(lgf * (rv + 1.0))
    xi_b = jnp.exp(lgb * (c - rv))
    g_f = jnp.exp(jnp.full((c, RET_V_DIM), lgf * c, F32))
    g_b = jnp.exp(jnp.full((c, RET_V_DIM), lgb * c, F32))

    pc = lax.broadcasted_iota(jnp.int32, (CTX_LEN, RET_QK_DIM), 0).astype(F32)
    kc = kc_ref[...].astype(F32)
    vc = vc_ref[...]
    s0_f = _dot_tn((kc * jnp.exp(lgf * (CTX_LEN - 1 - pc))).astype(BF16), vc)
    s0_b = _dot_tn((kc * jnp.exp(lgb * pc)).astype(BF16), vc)

    def chunk(n):
        return pl.ds(pl.multiple_of(n * c, c), c)

    def bwd_one(n, s):
        sb_ref[n] = s.astype(BF16)
        kz = (k_ref[chunk(n), :].astype(F32) * zeta_b).astype(BF16)
        return g_b * s + _dot_tn(kz, v_ref[chunk(n), :])

    def bwd(t, s):
        for u in range(RET_UNROLL):
            s = bwd_one(N_CHUNKS - 1 - (t * RET_UNROLL + u), s)
        return s

    lax.fori_loop(0, N_CHUNKS // RET_UNROLL, bwd, s0_b)

    def fwd_one(n, s):
        q = q_ref[chunk(n), :]
        k = k_ref[chunk(n), :]
        v = v_ref[chunk(n), :]
        scores = (_dot_nt(q, k) * decay).astype(BF16)
        o = (_dot(scores, v)
             + _dot(q, s.astype(BF16)) * xi_f
             + _dot(q, sb_ref[n]) * xi_b)
        mu = jnp.mean(o, axis=-1, keepdims=True)
        oc = o - mu
        var = jnp.mean(oc * oc, axis=-1, keepdims=True)
        gn = oc * lax.rsqrt(var + GN_EPS)
        o_ref[chunk(n), :] = (g_ref[chunk(n), :].astype(F32) * gn).astype(BF16)
        kz = (k.astype(F32) * zeta_f).astype(BF16)
        return g_f * s + _dot_tn(kz, v)

    def fwd(t, s):
        for u in range(RET_UNROLL):
            s = fwd_one(t * RET_UNROLL + u, s)
        return s

    lax.fori_loop(0, N_CHUNKS // RET_UNROLL, fwd, s0_f)


def _ret(log_gamma, rproj, proj, cproj):
    qk = lambda off: off // RET_QK_DIM
    vv = lambda off: off // RET_V_DIM
    return pl.pallas_call(
        _ret_kernel,
        out_shape=jax.ShapeDtypeStruct((N_LAT, RET_V_W), BF16),
        grid=(BATCH, RET_HEADS),
        in_specs=[
            pl.BlockSpec(memory_space=pltpu.SMEM),
            pl.BlockSpec((SEQ, RET_QK_DIM), lambda b, h: (b, qk(ROPE_QR) + h)),
            pl.BlockSpec((SEQ, RET_QK_DIM), lambda b, h: (b, qk(ROPE_KR) + h)),
            pl.BlockSpec((SEQ, RET_V_DIM), lambda b, h: (b, vv(REST_VR) + h)),
            pl.BlockSpec((SEQ, RET_V_DIM), lambda b, h: (b, vv(REST_GR) + h)),
            pl.BlockSpec((CTX_LEN, RET_QK_DIM), lambda b, h: (b, qk(OFF_KR) + h)),
            pl.BlockSpec((CTX_LEN, RET_V_DIM), lambda b, h: (b, vv(OFF_VR) + h)),
        ],
        out_specs=pl.BlockSpec((SEQ, RET_V_DIM), lambda b, h: (b, h)),
        scratch_shapes=[pltpu.VMEM((N_CHUNKS, RET_QK_DIM, RET_V_DIM), BF16)],
        compiler_params=_params("parallel", "parallel"),
        name="retention",
    )(log_gamma, rproj, rproj, proj, proj, cproj, cproj)


def _merge_kernel(ona_ref, ret_ref, gna0_ref, gna1_ref, gr0_ref, gr1_ref, x_ref, mod_ref,
                  wna_ref, wret_ref, wout_ref, lng_ref, lnb_ref, o_ref):
    y_na = _dot(ona_ref[...], wna_ref[...])
    y_ret = _dot(ret_ref[...], wret_ref[...])
    gate_na = jnp.concatenate([gna0_ref[...], gna1_ref[...]], axis=1).astype(F32)
    gate_ret = jnp.concatenate([gr0_ref[...], gr1_ref[...]], axis=1).astype(F32)
    z = (gate_na * y_na + gate_ret * y_ret).astype(BF16)
    y = _dot(z, wout_ref[...])
    g = mod_ref[5:6, :]
    o_ref[...] = _layer_norm(ALPHA * x_ref[...] + g * y, lng_ref[...], lnb_ref[...], LN_EPS)


def _merge(o_na, ret, proj, x, mod, w_na, w_ret, w_out, ln_g, ln_b):
    tm = MERGE_ROWS
    tiles_per_batch = SEQ // tm
    half = D_MODEL // 2
    gate = lambda off: pl.BlockSpec((tm, half), lambda i: (i, off // half))
    resident = lambda shape: pl.BlockSpec(shape, lambda i: (0, 0), pipeline_mode=pl.Buffered(1))
    return pl.pallas_call(
        _merge_kernel,
        out_shape=jax.ShapeDtypeStruct((N_LAT, D_MODEL), F32),
        grid=(N_LAT // tm,),
        in_specs=[
            pl.BlockSpec((tm, NA_W), lambda i: (i, 0)),
            pl.BlockSpec((tm, RET_V_W), lambda i: (i, 0)),
            gate(REST_GNA), gate(REST_GNA + half), gate(REST_GRET), gate(REST_GRET + half),
            pl.BlockSpec((tm, D_MODEL), lambda i: (i, 0)),
            pl.BlockSpec((None, N_MOD, D_MODEL), lambda i: (i // tiles_per_batch, 0, 0)),
            resident((NA_W, D_MODEL)),
            resident((RET_V_W, D_MODEL)),
            resident((D_MODEL, D_MODEL)),
            pl.BlockSpec((1, D_MODEL), lambda i: (0, 0)),
            pl.BlockSpec((1, D_MODEL), lambda i: (0, 0)),
        ],
        out_specs=pl.BlockSpec((tm, D_MODEL), lambda i: (i, 0)),
        compiler_params=_params("parallel"),
        name="merge",
    )(o_na, ret, proj, proj, proj, proj, x, mod, w_na, w_ret, w_out, ln_g, ln_b)


def _rope_tables():
    nfreq = RET_QK_DIM // 4
    pos = jnp.arange(SEQ)
    prow = (pos // GRID_W).astype(F32)
    pcol = (pos % GRID_W).astype(F32)
    freqs = ROPE_BASE ** (-jnp.arange(nfreq, dtype=F32) / nfreq)
    ang = jnp.stack([prow, pcol], axis=-1)[:, :, None] * freqs
    cos = jnp.cos(ang)
    sin = jnp.sin(ang)
    cos_t = jnp.stack([cos, cos], axis=2).reshape(SEQ, RET_QK_DIM)
    sin_t = jnp.stack([-sin, sin], axis=2).reshape(SEQ, RET_QK_DIM)
    return cos_t, sin_t


def _na_bias_table(rpb):
    wc = NA_WIN_COLS
    col = jnp.arange(GRID_W)
    col_start = jnp.clip(col - wc // 2, 0, GRID_W - wc)
    col_mask = (col[None, :] >= col_start[:, None]) & (col[None, :] < col_start[:, None] + wc)
    dc_idx = jnp.clip(col[None, :] - col[:, None] + wc - 1, 0, 2 * wc - 2)
    t = jnp.where(col_mask[None, None], rpb[:, :, dc_idx], NEG_INF)
    last = NA_WIN_ROWS - 1
    tb = jnp.stack([t[:, last - d:last - d + NA_WIN_ROWS] for d in range(NA_WIN_ROWS)], axis=1)
    tb = tb.reshape(NA_HEADS // 2, 2, NA_WIN_ROWS, NA_WIN_ROWS, GRID_W, GRID_W)
    return tb.transpose(0, 2, 1, 4, 3, 5).reshape(NA_HEADS // 2, NA_WIN_ROWS, 2 * GRID_W, NA_WIN_ROWS * GRID_W)


def _column_table(n_cols, groups):
    col = jnp.arange(n_cols)
    kind = jnp.full((n_cols,), KIND_SCALE, jnp.int32)
    scale = jnp.ones((n_cols,), F32)
    for lo, hi, k, s in groups:
        inside = (col >= lo) & (col < hi)
        kind = jnp.where(inside, k, kind)
        scale = jnp.where(inside, s, scale)
    return kind.reshape(1, n_cols), scale.reshape(1, n_cols)


def kernel(x, c, ctx, c_ctx, w_ada, b_ada, ln_g, ln_b, ffn1_w_in, ffn1_w_out, ffn2_w_in, ffn2_w_out,
           w_mix_in, na_rpb, ret_decay_fwd, ret_decay_bwd, w_na_out, w_ret_out, w_mix_out):
    assert x.shape == (BATCH, SEQ, D_MODEL) and ctx.shape == (BATCH, CTX_LEN, D_MODEL)
    assert w_ada.shape == (DEPTH, D_MODEL, N_MOD * D_MODEL) and w_mix_in.shape == (DEPTH, D_MODEL, MIX_COLS)
    l = 0
    ctx_mod = BATCH
    cc = jnp.zeros((MOD_ROWS, D_MODEL), F32).at[:BATCH].set(c).at[ctx_mod].set(c_ctx)
    mod = _adaln(cc, w_ada[l], b_ada[l].reshape(1, -1)).reshape(MOD_ROWS, N_MOD, D_MODEL)

    row = lambda v: v.reshape(1, D_MODEL)
    w1_in, w1_out = ffn1_w_in[l].astype(BF16), ffn1_w_out[l].astype(BF16)
    w2_in, w2_out = ffn2_w_in[l].astype(BF16), ffn2_w_out[l].astype(BF16)
    wm = w_mix_in[l]
    w_ctx = wm[:, :KV_COLS].astype(BF16)
    w_rope = jnp.concatenate([wm[:, OFF_KR:OFF_VR], wm[:, OFF_QR:OFF_GR]], axis=1).astype(BF16)
    w_rest = jnp.concatenate([wm[:, OFF_KA:OFF_KR], wm[:, OFF_VR:OFF_QR], wm[:, OFF_GR:]], axis=1).astype(BF16)
    w_na, w_ret, w_out = w_na_out[l].astype(BF16), w_ret_out[l].astype(BF16), w_mix_out[l].astype(BF16)

    xf = x.reshape(N_LAT, D_MODEL)
    cf = ctx.reshape(N_CTX, D_MODEL)
    x1 = _ffn(xf, mod, w1_in, w1_out, row(ln_g[l, 0]), row(ln_b[l, 0]), k0=0, rows_per_mod=SEQ, mod_base=0)
    c1 = _ffn(cf, mod, w1_in, w1_out, row(ln_g[l, 0]), row(ln_b[l, 0]), k0=0, rows_per_mod=N_CTX, mod_base=ctx_mod)

    cos_t, sin_t = _rope_tables()
    rest_kind, rest_scale = _column_table(REST_COLS, [
        (REST_QA, REST_GR, KIND_SCALE, NA_HEAD_DIM ** -0.5),
        (REST_GR, REST_GNA, KIND_SILU, 1.0),
        (REST_GNA, REST_COLS, KIND_SIGMOID, 1.0)])
    ctx_kind, ctx_scale = _column_table(KV_COLS, [(OFF_KR, OFF_VR, KIND_SCALE, RET_QK_DIM ** -0.5)])
    rproj = _rope_proj(x1, mod, w_rope, cos_t, sin_t)
    proj = _proj(x1, mod, w_rest, rest_kind, rest_scale, n_cols=REST_COLS, tm=PROJ_ROWS, tn=PROJ_COLS,
                 rows_per_mod=SEQ, mod_base=0, gated=True)
    cproj = _proj(c1, mod, w_ctx, ctx_kind, ctx_scale, n_cols=KV_COLS, tm=N_CTX, tn=CTX_PROJ_COLS,
                  rows_per_mod=N_CTX, mod_base=ctx_mod, gated=False)

    o_na = _na(proj, cproj, _na_bias_table(na_rpb[l]))
    log_gamma = jnp.stack([jax.nn.log_sigmoid(ret_decay_fwd[l].astype(F32)),
                           jax.nn.log_sigmoid(ret_decay_bwd[l].astype(F32))])
    ret = _ret(log_gamma, rproj, proj, cproj)

    x2 = _merge(o_na, ret, proj, x1, mod, w_na, w_ret, w_out, row(ln_g[l, 1]), row(ln_b[l, 1]))
    x3 = _ffn(x2, mod, w2_in, w2_out, row(ln_g[l, 2]), row(ln_b[l, 2]), k0=6, rows_per_mod=SEQ, mod_base=0)
    return x3.reshape(BATCH, SEQ, D_MODEL)
```

```python
import functools

import jax
import jax.numpy as jnp
from jax import lax
from jax.experimental import pallas as pl
from jax.experimental.pallas import tpu as pltpu

D_MODEL = 2048
BATCH = 2
SEQ = 8192
GRID_W = 64
CTX_LEN = 256
NA_HEADS = 16
NA_HEAD_DIM = 64
NA_W = NA_HEADS * NA_HEAD_DIM
NA_WIN_ROWS = 8
NA_WIN_COLS = 16
RET_HEADS = 8
RET_QK_DIM = 128
RET_V_DIM = 256
RET_QK_W = RET_HEADS * RET_QK_DIM
RET_V_W = RET_HEADS * RET_V_DIM
RET_CHUNK = 128
D_FF = 5632
N_MOD = 9
ROPE_BASE = 10000.0
LN_EPS = 1e-5
GN_EPS = 1e-6
NEG_INF = -1e30
DEPTH = 1
ALPHA = (2 * DEPTH) ** 0.25
MIX_COLS = 2 * NA_W + RET_QK_W + RET_V_W + NA_W + RET_QK_W + RET_V_W + 2 * D_MODEL
KV_COLS = 2 * NA_W + RET_QK_W + RET_V_W

OFF_KA = 0
OFF_VA = NA_W
OFF_KR = 2 * NA_W
OFF_VR = OFF_KR + RET_QK_W
OFF_QA = OFF_VR + RET_V_W
OFF_QR = OFF_QA + NA_W
OFF_GR = OFF_QR + RET_QK_W
OFF_GNA = OFF_GR + RET_V_W
OFF_GRET = OFF_GNA + D_MODEL

ROPE_COLS = 2 * RET_QK_W
ROPE_KR = 0
ROPE_QR = RET_QK_W
REST_KA = 0
REST_VA = REST_KA + NA_W
REST_VR = REST_VA + NA_W
REST_QA = REST_VR + RET_V_W
REST_GR = REST_QA + NA_W
REST_GNA = REST_GR + RET_V_W
REST_GRET = REST_GNA + D_MODEL
REST_COLS = REST_GRET + D_MODEL

KIND_SCALE = 0
KIND_SILU = 1
KIND_SIGMOID = 2

N_LAT = BATCH * SEQ
N_CTX = BATCH * CTX_LEN
GRID_ROWS = SEQ // GRID_W
N_CHUNKS = SEQ // RET_CHUNK

V7X_VMEM_BYTES = 64 * 1024 * 1024
VMEM_LIMIT_BYTES = V7X_VMEM_BYTES - 8 * 1024 * 1024
LANES = 128

FFN_ROWS = 512
FFN_COLS = 512
PROJ_ROWS = 1024
PROJ_COLS = 1024
CTX_PROJ_COLS = 2560
MERGE_ROWS = 256
ADA_COLS = 1024
MOD_ROWS = 8
NA_GROUP = 2
RET_GROUP = 4
RET_UNROLL = 8

BF16 = jnp.bfloat16
F32 = jnp.float32


def _dot(a, b):
    return jnp.dot(a, b, preferred_element_type=F32)


def _dot_nt(a, b):
    return lax.dot_general(a, b, (((1,), (1,)), ((), ())), preferred_element_type=F32)


def _dot_tn(a, b):
    return lax.dot_general(a, b, (((0,), (0,)), ((), ())), preferred_element_type=F32)


def _sigmoid(x):
    return 1.0 / (1.0 + jnp.exp(-x))


def _layer_norm(y, g, b, eps):
    mu = jnp.mean(y, axis=-1, keepdims=True)
    yc = y - mu
    var = jnp.mean(yc * yc, axis=-1, keepdims=True)
    return yc * lax.rsqrt(var + eps) * g + b


def _params(*semantics):
    return pltpu.CompilerParams(dimension_semantics=semantics, vmem_limit_bytes=VMEM_LIMIT_BYTES)


def _adaln_kernel(c_ref, w_ref, b_ref, o_ref):
    cc = c_ref[...]
    s = (cc * _sigmoid(cc)).astype(BF16)
    o_ref[...] = _dot(s, w_ref[...].astype(BF16)) + b_ref[...]


def _adaln(cc, w_ada, b_ada):
    n_out = N_MOD * D_MODEL
    return pl.pallas_call(
        _adaln_kernel,
        out_shape=jax.ShapeDtypeStruct((MOD_ROWS, n_out), F32),
        grid=(n_out // ADA_COLS,),
        in_specs=[
            pl.BlockSpec((MOD_ROWS, D_MODEL), lambda j: (0, 0)),
            pl.BlockSpec((D_MODEL, ADA_COLS), lambda j: (0, j)),
            pl.BlockSpec((1, ADA_COLS), lambda j: (0, j)),
        ],
        out_specs=pl.BlockSpec((MOD_ROWS, ADA_COLS), lambda j: (0, j)),
        compiler_params=_params("parallel"),
        name="adaln",
    )(cc, w_ada, b_ada)


def _ffn_kernel(x_ref, mod_ref, wg_ref, wu_ref, wo_ref, lng_ref, lnb_ref, o_ref, h_ref, acc_ref, *, k0):
    j = pl.program_id(1)

    @pl.when(j == 0)
    def _():
        shift = mod_ref[k0:k0 + 1, :]
        scale = mod_ref[k0 + 1:k0 + 2, :]
        h_ref[...] = (x_ref[...] * (1.0 + scale) + shift).astype(BF16)
        acc_ref[...] = jnp.zeros_like(acc_ref)

    h = h_ref[...]
    gate = _dot(h, wg_ref[...])
    up = _dot(h, wu_ref[...])
    act = (gate * _sigmoid(gate) * up).astype(BF16)
    acc_ref[...] += _dot(act, wo_ref[...])

    @pl.when(j == pl.num_programs(1) - 1)
    def _():
        g = mod_ref[k0 + 2:k0 + 3, :]
        y = ALPHA * x_ref[...] + g * (0.5 * acc_ref[...])
        o_ref[...] = _layer_norm(y, lng_ref[...], lnb_ref[...], LN_EPS)


def _ffn(x, mod, w_in, w_out, ln_g, ln_b, *, k0, rows_per_mod, mod_base):
    n_rows = x.shape[0]
    tm, tf = FFN_ROWS, FFN_COLS
    tiles_per_mod = rows_per_mod // tm
    n_col = D_FF // tf
    return pl.pallas_call(
        functools.partial(_ffn_kernel, k0=k0),
        out_shape=jax.ShapeDtypeStruct((n_rows, D_MODEL), F32),
        grid=(n_rows // tm, n_col),
        in_specs=[
            pl.BlockSpec((tm, D_MODEL), lambda i, j: (i, 0)),
            pl.BlockSpec((None, N_MOD, D_MODEL), lambda i, j: (mod_base + i // tiles_per_mod, 0, 0)),
            pl.BlockSpec((D_MODEL, tf), lambda i, j: (0, j)),
            pl.BlockSpec((D_MODEL, tf), lambda i, j: (0, j + n_col)),
            pl.BlockSpec((tf, D_MODEL), lambda i, j: (j, 0)),
            pl.BlockSpec((1, D_MODEL), lambda i, j: (0, 0)),
            pl.BlockSpec((1, D_MODEL), lambda i, j: (0, 0)),
        ],
        out_specs=pl.BlockSpec((tm, D_MODEL), lambda i, j: (i, 0)),
        scratch_shapes=[pltpu.VMEM((tm, D_MODEL), BF16), pltpu.VMEM((tm, D_MODEL), F32)],
        compiler_params=_params("parallel", "arbitrary"),
        name="ffn",
    )(x, mod, w_in, w_in, w_out, ln_g, ln_b)


def _swap_pairs(x, lane_is_first):
    return jnp.where(lane_is_first, pltpu.roll(x, LANES - 32, 1), pltpu.roll(x, 32, 1))


def _mix_modulate(x_ref, mod_ref):
    shift = mod_ref[3:4, :]
    scale = mod_ref[4:5, :]
    return (x_ref[...] * (1.0 + scale) + shift).astype(BF16)


def _proj_kernel(x_ref, mod_ref, w_ref, kind_ref, scale_ref, o_ref, h_ref, *, gated):
    @pl.when(pl.program_id(1) == 0)
    def _():
        h_ref[...] = _mix_modulate(x_ref, mod_ref)

    acc = _dot(h_ref[...], w_ref[...])
    out = acc * scale_ref[...]
    if gated:
        kind = kind_ref[...]
        gate = _sigmoid(acc) * jnp.where(kind == KIND_SILU, acc, 1.0)
        out = jnp.where(kind == KIND_SCALE, out, gate)
    o_ref[...] = out.astype(BF16)


def _proj(x, mod, w, kind, scale, *, n_cols, tm, tn, rows_per_mod, mod_base, gated, w_block=lambda j: j):
    n_rows = x.shape[0]
    tiles_per_mod = rows_per_mod // tm
    return pl.pallas_call(
        functools.partial(_proj_kernel, gated=gated),
        out_shape=jax.ShapeDtypeStruct((n_rows, n_cols), BF16),
        grid=(n_rows // tm, n_cols // tn),
        in_specs=[
            pl.BlockSpec((tm, D_MODEL), lambda i, j: (i, 0)),
            pl.BlockSpec((None, N_MOD, D_MODEL), lambda i, j: (mod_base + i // tiles_per_mod, 0, 0)),
            pl.BlockSpec((D_MODEL, tn), lambda i, j: (0, w_block(j))),
            pl.BlockSpec((1, tn), lambda i, j: (0, j)),
            pl.BlockSpec((1, tn), lambda i, j: (0, j)),
        ],
        out_specs=pl.BlockSpec((tm, tn), lambda i, j: (i, j)),
        scratch_shapes=[pltpu.VMEM((tm, D_MODEL), BF16)],
        compiler_params=_params("parallel", "arbitrary"),
        name="mix_proj",
    )(x, mod, w, kind, scale)


def _rope_proj_kernel(x_ref, mod_ref, wk_ref, wq_ref, cos_ref, sin_ref, o_ref):
    h = _mix_modulate(x_ref, mod_ref)
    cos = cos_ref[...]
    sin = sin_ref[...]
    lane = lax.broadcasted_iota(jnp.int32, (1, LANES), 1)
    first = (lane & 32) == 0
    for w_ref, base, scale in ((wk_ref, ROPE_KR, RET_QK_DIM ** -0.5), (wq_ref, ROPE_QR, None)):
        acc = _dot(h, w_ref[...])
        for hd in range(RET_HEADS):
            xh = acc[:, hd * LANES:(hd + 1) * LANES]
            r = xh * cos + _swap_pairs(xh, first) * sin
            if scale is not None:
                r = r * scale
            o_ref[:, base + hd * LANES:base + (hd + 1) * LANES] = r.astype(BF16)


def _rope_proj(x, mod, w, cos, sin):
    tm = PROJ_ROWS
    tiles_per_batch = SEQ // tm
    w_cols = lambda off: pl.BlockSpec((D_MODEL, RET_QK_W), lambda i: (0, off // RET_QK_W),
                                      pipeline_mode=pl.Buffered(1))
    return pl.pallas_call(
        _rope_proj_kernel,
        out_shape=jax.ShapeDtypeStruct((N_LAT, ROPE_COLS), BF16),
        grid=(N_LAT // tm,),
        in_specs=[
            pl.BlockSpec((tm, D_MODEL), lambda i: (i, 0)),
            pl.BlockSpec((None, N_MOD, D_MODEL), lambda i: (i // tiles_per_batch, 0, 0)),
            w_cols(OFF_KR),
            w_cols(OFF_QR),
            pl.BlockSpec((tm, LANES), lambda i: (i % tiles_per_batch, 0)),
            pl.BlockSpec((tm, LANES), lambda i: (i % tiles_per_batch, 0)),
        ],
        out_specs=pl.BlockSpec((tm, ROPE_COLS), lambda i: (i, 0)),
        compiler_params=_params("parallel"),
        name="rope_proj",
    )(x, mod, w, w, cos, sin)


def _na_kernel(q_ref, k_ref, v_ref, kc_ref, vc_ref, bias_ref, o_ref, s0_ref, c0_ref, s1_ref, c1_ref):
    lane = lax.broadcasted_iota(jnp.int32, (1, LANES), 1)
    first_head = lane < NA_HEAD_DIM
    zero = jnp.zeros((), BF16)
    kc = kc_ref[...]
    vc = vc_ref[...]
    win = NA_WIN_ROWS * GRID_W
    rows2 = 2 * GRID_W
    n_groups = GRID_ROWS // NA_GROUP
    slots = ((s0_ref, c0_ref), (s1_ref, c1_ref))

    def window_start(r):
        return jnp.clip(r - NA_WIN_ROWS // 2, 0, GRID_ROWS - NA_WIN_ROWS)

    def tokens(r, n):
        return pl.ds(pl.multiple_of(r * GRID_W, GRID_W), n)

    def scores(t, slot):
        s_ref, c_ref = slots[slot]
        q2s = []
        for u in range(NA_GROUP):
            r = t * NA_GROUP + u
            q = q_ref[tokens(r, GRID_W), :]
            q2 = jnp.concatenate([jnp.where(first_head, q, zero), jnp.where(first_head, zero, q)], axis=0)
            s_ref[u] = _dot_nt(q2, k_ref[tokens(window_start(r), win), :])
            q2s.append(q2)
        c_ref[...] = _dot_nt(jnp.concatenate(q2s, axis=0), kc)

    def attend(t, slot):
        s_ref, c_ref = slots[slot]
        o_lat, p_ctx, den = [], [], []
        for u in range(NA_GROUP):
            r = t * NA_GROUP + u
            rs = window_start(r)
            delta = r - rs
            bias = jnp.concatenate(
                [bias_ref[2 * m - delta + NA_WIN_ROWS - 1] for m in range(NA_WIN_ROWS // 2)], axis=1)
            s_lat = s_ref[u] + bias
            s_ctx = c_ref[u * rows2:(u + 1) * rows2, :]
            mx = jnp.maximum(jnp.max(s_lat, axis=-1, keepdims=True), jnp.max(s_ctx, axis=-1, keepdims=True))
            e_lat = jnp.exp(s_lat - mx)
            e_ctx = jnp.exp(s_ctx - mx)
            den.append(jnp.sum(e_lat, axis=-1, keepdims=True) + jnp.sum(e_ctx, axis=-1, keepdims=True))
            o_lat.append(_dot(e_lat.astype(BF16), v_ref[tokens(rs, win), :]))
            p_ctx.append(e_ctx.astype(BF16))
        o_ctx = _dot(jnp.concatenate(p_ctx, axis=0), vc)
        for u in range(NA_GROUP):
            o2 = (o_lat[u] + o_ctx[u * rows2:(u + 1) * rows2]) / den[u]
            out = jnp.where(first_head, o2[:GRID_W], o2[GRID_W:])
            o_ref[tokens(t * NA_GROUP + u, GRID_W), :] = out.astype(BF16)

    scores(0, 0)

    def two_groups(i, carry):
        scores(2 * i + 1, 1)
        attend(2 * i, 0)
        scores(2 * i + 2, 0)
        attend(2 * i + 1, 1)
        return carry

    lax.fori_loop(0, n_groups // 2 - 1, two_groups, 0)
    scores(n_groups - 1, 1)
    attend(n_groups - 2, 0)
    attend(n_groups - 1, 1)


def _na(proj, cproj, bias):
    n_pairs = NA_HEADS // 2
    blk = lambda off: off // LANES
    return pl.pallas_call(
        _na_kernel,
        out_shape=jax.ShapeDtypeStruct((N_LAT, NA_W), BF16),
        grid=(BATCH, n_pairs),
        in_specs=[
            pl.BlockSpec((SEQ, LANES), lambda b, p: (b, blk(REST_QA) + p)),
            pl.BlockSpec((SEQ, LANES), lambda b, p: (b, blk(REST_KA) + p)),
            pl.BlockSpec((SEQ, LANES), lambda b, p: (b, blk(REST_VA) + p)),
            pl.BlockSpec((CTX_LEN, LANES), lambda b, p: (b, blk(OFF_KA) + p)),
            pl.BlockSpec((CTX_LEN, LANES), lambda b, p: (b, blk(OFF_VA) + p)),
            pl.BlockSpec((None, 2 * NA_WIN_ROWS - 2, 2 * GRID_W, 2 * GRID_W), lambda b, p: (p, 0, 0, 0)),
        ],
        out_specs=pl.BlockSpec((SEQ, LANES), lambda b, p: (b, p)),
        scratch_shapes=2 * [pltpu.VMEM((NA_GROUP, 2 * GRID_W, NA_WIN_ROWS * GRID_W), F32),
                            pltpu.VMEM((NA_GROUP * 2 * GRID_W, CTX_LEN), F32)],
        compiler_params=_params("parallel", "parallel"),
        name="nbr_attn",
    )(proj, proj, proj, cproj, cproj, bias)


def _ret_kernel(lg_ref, q_ref, k_ref, v_ref, g_ref, kc_ref, vc_ref, o_ref, sb_ref, kvf_ref, a0_ref, a1_ref):
    h = pl.program_id(1)
    lgf = lg_ref[0, h]
    lgb = lg_ref[1, h]
    c = RET_CHUNK
    ri = lax.broadcasted_iota(jnp.int32, (c, c), 0).astype(F32)
    ci = lax.broadcasted_iota(jnp.int32, (c, c), 1).astype(F32)
    diff = ri - ci
    decay = (jnp.where(diff >= 0, jnp.exp(lgf * jnp.maximum(diff, 0.0)), 0.0)
             + jnp.where(diff < 0, jnp.exp(lgb * jnp.maximum(-diff, 0.0)), 0.0))
    zeta_f = jnp.exp(lgf * (c - 1 - ci))
    zeta_b = jnp.exp(lgb * ci)
    rv = lax.broadcasted_iota(jnp.int32, (c, RET_V_DIM), 0).astype(F32)
    xi_f = jnp.exp(lgf * (rv + 1.0))
    xi_b = jnp.exp(lgb * (c - rv))
    g_f = jnp.exp(jnp.full((c, RET_V_DIM), lgf * c, F32))
    g_b = jnp.exp(jnp.full((c, RET_V_DIM), lgb * c, F32))

    pc = lax.broadcasted_iota(jnp.int32, (CTX_LEN, RET_QK_DIM), 0).astype(F32)
    kc = kc_ref[...].astype(F32)
    vc = vc_ref[...]
    s0_f = _dot_tn((kc * jnp.exp(lgf * (CTX_LEN - 1 - pc))).astype(BF16), vc)
    s0_b = _dot_tn((kc * jnp.exp(lgb * pc)).astype(BF16), vc)

    def chunk(n):
        return pl.ds(pl.multiple_of(n * c, c), c)

    def bwd_one(n, s):
        sb_ref[n] = s.astype(BF16)
        kt = k_ref[chunk(n), :].astype(F32).T
        v = v_ref[chunk(n), :]
        kvf_ref[n] = _dot((kt * zeta_f).astype(BF16), v)
        return g_b * s + _dot((kt * zeta_b).astype(BF16), v)

    def bwd(t, s):
        for u in range(RET_UNROLL):
            s = bwd_one(N_CHUNKS - 1 - (t * RET_UNROLL + u), s)
        return s

    lax.fori_loop(0, N_CHUNKS // RET_UNROLL, bwd, s0_b)

    n_groups = N_CHUNKS // RET_GROUP

    def fwd_scores(t, a_ref):
        for u in range(RET_GROUP):
            n = t * RET_GROUP + u
            a_ref[u] = _dot_nt(q_ref[chunk(n), :], k_ref[chunk(n), :])

    def fwd_out(t, a_ref, s):
        for u in range(RET_GROUP):
            n = t * RET_GROUP + u
            q = q_ref[chunk(n), :]
            scores = (a_ref[u] * decay).astype(BF16)
            cross = _dot(q, jnp.concatenate([s.astype(BF16), sb_ref[n]], axis=1))
            o = (_dot(scores, v_ref[chunk(n), :])
                 + cross[:, :RET_V_DIM] * xi_f
                 + cross[:, RET_V_DIM:] * xi_b)
            mu = jnp.mean(o, axis=-1, keepdims=True)
            oc = o - mu
            var = jnp.mean(oc * oc, axis=-1, keepdims=True)
            gn = oc * lax.rsqrt(var + GN_EPS)
            o_ref[chunk(n), :] = (g_ref[chunk(n), :].astype(F32) * gn).astype(BF16)
            s = g_f * s + kvf_ref[n]
        return s

    fwd_scores(0, a0_ref)

    def two_groups(i, s):
        fwd_scores(2 * i + 1, a1_ref)
        s = fwd_out(2 * i, a0_ref, s)
        fwd_scores(2 * i + 2, a0_ref)
        return fwd_out(2 * i + 1, a1_ref, s)

    s = lax.fori_loop(0, n_groups // 2 - 1, two_groups, s0_f)
    fwd_scores(n_groups - 1, a1_ref)
    s = fwd_out(n_groups - 2, a0_ref, s)
    fwd_out(n_groups - 1, a1_ref, s)


def _ret(log_gamma, rproj, proj, cproj):
    qk = lambda off: off // RET_QK_DIM
    vv = lambda off: off // RET_V_DIM
    return pl.pallas_call(
        _ret_kernel,
        out_shape=jax.ShapeDtypeStruct((N_LAT, RET_V_W), BF16),
        grid=(BATCH, RET_HEADS),
        in_specs=[
            pl.BlockSpec(memory_space=pltpu.SMEM),
            pl.BlockSpec((SEQ, RET_QK_DIM), lambda b, h: (b, qk(ROPE_QR) + h)),
            pl.BlockSpec((SEQ, RET_QK_DIM), lambda b, h: (b, qk(ROPE_KR) + h)),
            pl.BlockSpec((SEQ, RET_V_DIM), lambda b, h: (b, vv(REST_VR) + h)),
            pl.BlockSpec((SEQ, RET_V_DIM), lambda b, h: (b, vv(REST_GR) + h)),
            pl.BlockSpec((CTX_LEN, RET_QK_DIM), lambda b, h: (b, qk(OFF_KR) + h)),
            pl.BlockSpec((CTX_LEN, RET_V_DIM), lambda b, h: (b, vv(OFF_VR) + h)),
        ],
        out_specs=pl.BlockSpec((SEQ, RET_V_DIM), lambda b, h: (b, h)),
        scratch_shapes=[pltpu.VMEM((N_CHUNKS, RET_QK_DIM, RET_V_DIM), BF16),
                        pltpu.VMEM((N_CHUNKS, RET_QK_DIM, RET_V_DIM), F32),
                        pltpu.VMEM((RET_GROUP, RET_CHUNK, RET_CHUNK), F32),
                        pltpu.VMEM((RET_GROUP, RET_CHUNK, RET_CHUNK), F32)],
        compiler_params=_params("parallel", "parallel"),
        name="retention",
    )(log_gamma, rproj, rproj, proj, proj, cproj, cproj)


def _merge_kernel(ona_ref, ret_ref, gna0_ref, gna1_ref, gr0_ref, gr1_ref, x_ref, mod_ref,
                  wna_ref, wret_ref, wout_ref, lng_ref, lnb_ref, o_ref):
    y_na = _dot(ona_ref[...], wna_ref[...])
    y_ret = _dot(ret_ref[...], wret_ref[...])
    gate_na = jnp.concatenate([gna0_ref[...], gna1_ref[...]], axis=1).astype(F32)
    gate_ret = jnp.concatenate([gr0_ref[...], gr1_ref[...]], axis=1).astype(F32)
    z = (gate_na * y_na + gate_ret * y_ret).astype(BF16)
    y = _dot(z, wout_ref[...])
    g = mod_ref[5:6, :]
    o_ref[...] = _layer_norm(ALPHA * x_ref[...] + g * y, lng_ref[...], lnb_ref[...], LN_EPS)


def _merge(o_na, ret, proj, x, mod, w_na, w_ret, w_out, ln_g, ln_b):
    tm = MERGE_ROWS
    tiles_per_batch = SEQ // tm
    half = D_MODEL // 2
    gate = lambda off: pl.BlockSpec((tm, half), lambda i: (i, off // half))
    resident = lambda shape: pl.BlockSpec(shape, lambda i: (0, 0), pipeline_mode=pl.Buffered(1))
    return pl.pallas_call(
        _merge_kernel,
        out_shape=jax.ShapeDtypeStruct((N_LAT, D_MODEL), F32),
        grid=(N_LAT // tm,),
        in_specs=[
            pl.BlockSpec((tm, NA_W), lambda i: (i, 0)),
            pl.BlockSpec((tm, RET_V_W), lambda i: (i, 0)),
            gate(REST_GNA), gate(REST_GNA + half), gate(REST_GRET), gate(REST_GRET + half),
            pl.BlockSpec((tm, D_MODEL), lambda i: (i, 0)),
            pl.BlockSpec((None, N_MOD, D_MODEL), lambda i: (i // tiles_per_batch, 0, 0)),
            resident((NA_W, D_MODEL)),
            resident((RET_V_W, D_MODEL)),
            resident((D_MODEL, D_MODEL)),
            pl.BlockSpec((1, D_MODEL), lambda i: (0, 0)),
            pl.BlockSpec((1, D_MODEL), lambda i: (0, 0)),
        ],
        out_specs=pl.BlockSpec((tm, D_MODEL), lambda i: (i, 0)),
        compiler_params=_params("parallel"),
        name="merge",
    )(o_na, ret, proj, proj, proj, proj, x, mod, w_na, w_ret, w_out, ln_g, ln_b)


def _rope_tables():
    nfreq = RET_QK_DIM // 4
    pos = jnp.arange(SEQ)
    prow = (pos // GRID_W).astype(F32)
    pcol = (pos % GRID_W).astype(F32)
    freqs = ROPE_BASE ** (-jnp.arange(nfreq, dtype=F32) / nfreq)
    ang = jnp.stack([prow, pcol], axis=-1)[:, :, None] * freqs
    cos = jnp.cos(ang)
    sin = jnp.sin(ang)
    cos_t = jnp.stack([cos, cos], axis=2).reshape(SEQ, RET_QK_DIM)
    sin_t = jnp.stack([-sin, sin], axis=2).reshape(SEQ, RET_QK_DIM)
    return cos_t, sin_t


def _na_bias_table(rpb):
    wc = NA_WIN_COLS
    n_dc = 2 * wc - 1
    col = jnp.arange(GRID_W)
    col_start = jnp.clip(col - wc // 2, 0, GRID_W - wc)
    col_mask = (col[None, :] >= col_start[:, None]) & (col[None, :] < col_start[:, None] + wc)
    dc_idx = jnp.clip(col[None, :] - col[:, None] + wc - 1, 0, n_dc - 1)
    onehot = ((dc_idx[None] == jnp.arange(n_dc)[:, None, None]) & col_mask[None]).astype(F32)
    t = jnp.einsum('hdc,cqk->hdqk', rpb, onehot, precision=lax.Precision.HIGHEST)
    t = t + jnp.where(col_mask, 0.0, NEG_INF)
    n_dr = 2 * NA_WIN_ROWS - 1
    t = t.reshape(NA_HEADS // 2, 2, n_dr, GRID_W, GRID_W).transpose(0, 2, 1, 3, 4)
    t = t.reshape(NA_HEADS // 2, n_dr, 2 * GRID_W, GRID_W)
    return jnp.concatenate([t[:, :-1], t[:, 1:]], axis=-1)


def _column_table(n_cols, groups):
    col = jnp.arange(n_cols)
    kind = jnp.full((n_cols,), KIND_SCALE, jnp.int32)
    scale = jnp.ones((n_cols,), F32)
    for lo, hi, k, s in groups:
        inside = (col >= lo) & (col < hi)
        kind = jnp.where(inside, k, kind)
        scale = jnp.where(inside, s, scale)
    return kind.reshape(1, n_cols), scale.reshape(1, n_cols)


def kernel(x, c, ctx, c_ctx, w_ada, b_ada, ln_g, ln_b, ffn1_w_in, ffn1_w_out, ffn2_w_in, ffn2_w_out,
           w_mix_in, na_rpb, ret_decay_fwd, ret_decay_bwd, w_na_out, w_ret_out, w_mix_out):
    assert x.shape == (BATCH, SEQ, D_MODEL) and ctx.shape == (BATCH, CTX_LEN, D_MODEL)
    assert w_ada.shape == (DEPTH, D_MODEL, N_MOD * D_MODEL) and w_mix_in.shape == (DEPTH, D_MODEL, MIX_COLS)
    l = 0
    ctx_mod = BATCH
    cc = jnp.zeros((MOD_ROWS, D_MODEL), F32).at[:BATCH].set(c).at[ctx_mod].set(c_ctx)
    mod = _adaln(cc, w_ada[l], b_ada[l].reshape(1, -1)).reshape(MOD_ROWS, N_MOD, D_MODEL)

    row = lambda v: v.reshape(1, D_MODEL)
    w1_in, w1_out = ffn1_w_in[l].astype(BF16), ffn1_w_out[l].astype(BF16)
    w2_in, w2_out = ffn2_w_in[l].astype(BF16), ffn2_w_out[l].astype(BF16)
    w_mix = w_mix_in[l].astype(BF16)
    w_na, w_ret, w_out = w_na_out[l].astype(BF16), w_ret_out[l].astype(BF16), w_mix_out[l].astype(BF16)

    xf = x.reshape(N_LAT, D_MODEL)
    cf = ctx.reshape(N_CTX, D_MODEL)
    x1 = _ffn(xf, mod, w1_in, w1_out, row(ln_g[l, 0]), row(ln_b[l, 0]), k0=0, rows_per_mod=SEQ, mod_base=0)
    c1 = _ffn(cf, mod, w1_in, w1_out, row(ln_g[l, 0]), row(ln_b[l, 0]), k0=0, rows_per_mod=N_CTX, mod_base=ctx_mod)

    cos_t, sin_t = _rope_tables()
    rest_kind, rest_scale = _column_table(REST_COLS, [
        (REST_QA, REST_GR, KIND_SCALE, NA_HEAD_DIM ** -0.5),
        (REST_GR, REST_GNA, KIND_SILU, 1.0),
        (REST_GNA, REST_COLS, KIND_SIGMOID, 1.0)])
    ctx_kind, ctx_scale = _column_table(KV_COLS, [(OFF_KR, OFF_VR, KIND_SCALE, RET_QK_DIM ** -0.5)])
    rproj = _rope_proj(x1, mod, w_mix, cos_t, sin_t)
    skip_rope = lambda j: (j + (j >= OFF_KR // PROJ_COLS).astype(jnp.int32)
                           + (j >= OFF_QR // PROJ_COLS - 1).astype(jnp.int32))
    proj = _proj(x1, mod, w_mix, rest_kind, rest_scale, n_cols=REST_COLS, tm=PROJ_ROWS, tn=PROJ_COLS,
                 rows_per_mod=SEQ, mod_base=0, gated=True, w_block=skip_rope)
    cproj = _proj(c1, mod, w_mix, ctx_kind, ctx_scale, n_cols=KV_COLS, tm=N_CTX, tn=CTX_PROJ_COLS,
                  rows_per_mod=N_CTX, mod_base=ctx_mod, gated=False)

    o_na = _na(proj, cproj, _na_bias_table(na_rpb[l]))
    log_gamma = jnp.stack([jax.nn.log_sigmoid(ret_decay_fwd[l].astype(F32)),
                           jax.nn.log_sigmoid(ret_decay_bwd[l].astype(F32))])
    ret = _ret(log_gamma, rproj, proj, cproj)

    x2 = _merge(o_na, ret, proj, x1, mod, w_na, w_ret, w_out, row(ln_g[l, 1]), row(ln_b[l, 1]))
    x3 = _ffn(x2, mod, w2_in, w2_out, row(ln_g[l, 2]), row(ln_b[l, 2]), k0=6, rows_per_mod=SEQ, mod_base=0)
    return x3.reshape(BATCH, SEQ, D_MODEL)
```

```python
import functools

import jax
import jax.numpy as jnp
from jax import lax
from jax.experimental import pallas as pl
from jax.experimental.pallas import tpu as pltpu

D_MODEL = 2048
BATCH = 2
SEQ = 8192
GRID_W = 64
CTX_LEN = 256
NA_HEADS = 16
NA_HEAD_DIM = 64
NA_W = NA_HEADS * NA_HEAD_DIM
NA_WIN_ROWS = 8
NA_WIN_COLS = 16
RET_HEADS = 8
RET_QK_DIM = 128
RET_V_DIM = 256
RET_QK_W = RET_HEADS * RET_QK_DIM
RET_V_W = RET_HEADS * RET_V_DIM
RET_CHUNK = 128
D_FF = 5632
N_MOD = 9
ROPE_BASE = 10000.0
LN_EPS = 1e-5
GN_EPS = 1e-6
NEG_INF = -1e30
DEPTH = 1
ALPHA = (2 * DEPTH) ** 0.25
MIX_COLS = 2 * NA_W + RET_QK_W + RET_V_W + NA_W + RET_QK_W + RET_V_W + 2 * D_MODEL
KV_COLS = 2 * NA_W + RET_QK_W + RET_V_W

OFF_KA = 0
OFF_VA = NA_W
OFF_KR = 2 * NA_W
OFF_VR = OFF_KR + RET_QK_W
OFF_QA = OFF_VR + RET_V_W
OFF_QR = OFF_QA + NA_W
OFF_GR = OFF_QR + RET_QK_W
OFF_GNA = OFF_GR + RET_V_W
OFF_GRET = OFF_GNA + D_MODEL

ROPE_COLS = 2 * RET_QK_W
ROPE_KR = 0
ROPE_QR = RET_QK_W
PLAIN_KA = 0
PLAIN_VA = PLAIN_KA + NA_W
PLAIN_VR = PLAIN_VA + NA_W
PLAIN_QA = PLAIN_VR + RET_V_W
PLAIN_COLS = PLAIN_QA + NA_W
GATE_COLS = 2 * D_MODEL

KIND_SCALE = 0
KIND_SILU = 1
KIND_SIGMOID = 2

N_LAT = BATCH * SEQ
N_CTX = BATCH * CTX_LEN
GRID_ROWS = SEQ // GRID_W
N_CHUNKS = SEQ // RET_CHUNK

V7X_VMEM_BYTES = 64 * 1024 * 1024
VMEM_LIMIT_BYTES = V7X_VMEM_BYTES - 8 * 1024 * 1024
LANES = 128

FFN_ROWS = 512
FFN_COLS = 512
PROJ_ROWS = 1024
PROJ_COLS = 1024
CTX_PROJ_COLS = 2560
MERGE_ROWS = 256
ADA_COLS = 1024
MOD_ROWS = 8
NA_GROUP = 2
RET_GROUP = 4
RET_UNROLL = 8
CAST_ROWS_FFN = 16
CAST_ROWS_MERGE_IN = 32
CAST_ROWS_MERGE_OUT = 176

BF16 = jnp.bfloat16
F32 = jnp.float32


def _dot(a, b):
    return jnp.dot(a, b, preferred_element_type=F32)


def _dot_nt(a, b):
    return lax.dot_general(a, b, (((1,), (1,)), ((), ())), preferred_element_type=F32)


def _dot_tn(a, b):
    return lax.dot_general(a, b, (((0,), (0,)), ((), ())), preferred_element_type=F32)


def _sigmoid(x):
    return 0.5 * jnp.tanh(0.5 * x) + 0.5


def _layer_norm(y, g, b, eps):
    mu = jnp.mean(y, axis=-1, keepdims=True)
    yc = y - mu
    var = jnp.mean(yc * yc, axis=-1, keepdims=True)
    return yc * lax.rsqrt(var + eps) * g + b


def _params(*semantics):
    return pltpu.CompilerParams(dimension_semantics=semantics, vmem_limit_bytes=VMEM_LIMIT_BYTES)


def _cast_spec(w, rows, step, n_steps):
    n_blocks = w.shape[0] // rows
    assert w.shape[0] % rows == 0 and rows % 16 == 0 and n_blocks <= n_steps
    return pl.BlockSpec((rows, w.shape[1]), lambda *g: (jnp.minimum(step(*g), n_blocks - 1), 0))


def _cast_blocks(src_refs, dst_refs):
    for src, dst in zip(src_refs, dst_refs):
        dst[...] = src[...].astype(BF16)


def _adaln_kernel(c_ref, w_ref, b_ref, o_ref):
    cc = c_ref[...]
    s = (cc * _sigmoid(cc)).astype(BF16)
    o_ref[...] = _dot(s, w_ref[...].astype(BF16)) + b_ref[...]


def _adaln(cc, w_ada, b_ada):
    n_out = N_MOD * D_MODEL
    return pl.pallas_call(
        _adaln_kernel,
        out_shape=jax.ShapeDtypeStruct((MOD_ROWS, n_out), F32),
        grid=(n_out // ADA_COLS,),
        in_specs=[
            pl.BlockSpec((MOD_ROWS, D_MODEL), lambda j: (0, 0)),
            pl.BlockSpec((D_MODEL, ADA_COLS), lambda j: (0, j)),
            pl.BlockSpec((1, ADA_COLS), lambda j: (0, j)),
        ],
        out_specs=pl.BlockSpec((MOD_ROWS, ADA_COLS), lambda j: (0, j)),
        compiler_params=_params("parallel"),
        name="adaln",
    )(cc, w_ada, b_ada)


def _ffn_kernel(x_ref, mod_ref, wg_ref, wu_ref, wo_ref, lng_ref, lnb_ref, *rest, k0, n_cast):
    cast_src = rest[:n_cast]
    o_ref = rest[n_cast]
    cast_dst = rest[n_cast + 1:2 * n_cast + 1]
    h_ref = rest[2 * n_cast + 1]
    j = pl.program_id(1)

    @pl.when(j == 0)
    def _():
        shift = mod_ref[k0:k0 + 1, :]
        scale = mod_ref[k0 + 1:k0 + 2, :]
        h_ref[...] = (x_ref[...] * (1.0 + scale) + shift).astype(BF16)
        o_ref[...] = jnp.zeros_like(o_ref)

    h = h_ref[...]
    gate = _dot(h, wg_ref[...])
    up = _dot(h, wu_ref[...])
    act = (gate * _sigmoid(gate) * up).astype(BF16)
    o_ref[...] += _dot(act, wo_ref[...])
    _cast_blocks(cast_src, cast_dst)

    @pl.when(j == pl.num_programs(1) - 1)
    def _():
        g = mod_ref[k0 + 2:k0 + 3, :]
        y = ALPHA * x_ref[...] + g * (0.5 * o_ref[...])
        o_ref[...] = _layer_norm(y, lng_ref[...], lnb_ref[...], LN_EPS)


def _ffn(x, mod, w_in, w_out, ln_g, ln_b, *, k0, tm, rows_per_mod, mod_base, cast=()):
    n_rows = x.shape[0]
    tf = FFN_COLS
    tiles_per_mod = rows_per_mod // tm
    n_col = D_FF // tf
    n_steps = (n_rows // tm) * n_col
    cast_spec = lambda wr: _cast_spec(wr[0], wr[1], lambda i, j: i * n_col + j, n_steps)
    cast = tuple(cast)

    out = pl.pallas_call(
        functools.partial(_ffn_kernel, k0=k0, n_cast=len(cast)),
        out_shape=[jax.ShapeDtypeStruct((n_rows, D_MODEL), F32)]
        + [jax.ShapeDtypeStruct(w.shape, BF16) for w, _ in cast],
        grid=(n_rows // tm, n_col),
        in_specs=[
            pl.BlockSpec((tm, D_MODEL), lambda i, j: (i, 0)),
            pl.BlockSpec((None, N_MOD, D_MODEL), lambda i, j: (mod_base + i // tiles_per_mod, 0, 0)),
            pl.BlockSpec((D_MODEL, tf), lambda i, j: (0, j)),
            pl.BlockSpec((D_MODEL, tf), lambda i, j: (0, j + n_col)),
            pl.BlockSpec((tf, D_MODEL), lambda i, j: (j, 0)),
            pl.BlockSpec((1, D_MODEL), lambda i, j: (0, 0)),
            pl.BlockSpec((1, D_MODEL), lambda i, j: (0, 0)),
        ] + [cast_spec(w) for w in cast],
        out_specs=[pl.BlockSpec((tm, D_MODEL), lambda i, j: (i, 0))] + [cast_spec(w) for w in cast],
        scratch_shapes=[pltpu.VMEM((tm, D_MODEL), BF16)],
        compiler_params=_params("arbitrary", "arbitrary"),
        name="ffn",
    )(x, mod, w_in, w_in, w_out, ln_g, ln_b, *[w for w, _ in cast])
    return out if cast else out[0]


def _swap_pairs(x, lane_is_first):
    return jnp.where(lane_is_first, pltpu.roll(x, LANES - 32, 1), pltpu.roll(x, 32, 1))


def _mix_modulate(x_ref, mod_ref):
    shift = mod_ref[3:4, :]
    scale = mod_ref[4:5, :]
    return (x_ref[...] * (1.0 + scale) + shift).astype(BF16)


def _proj_kernel(x_ref, mod_ref, w_ref, scale_ref, o_ref, h_ref, *, kind):
    @pl.when(pl.program_id(1) == 0)
    def _():
        h_ref[...] = _mix_modulate(x_ref, mod_ref)

    acc = _dot(h_ref[...], w_ref[...])
    if kind == KIND_SCALE:
        out = acc * scale_ref[...]
    elif kind == KIND_SILU:
        out = acc * _sigmoid(acc)
    else:
        out = _sigmoid(acc)
    o_ref[...] = out.astype(BF16)


def _proj(x, mod, w, scale, *, kind, n_cols, tm, tn, rows_per_mod, mod_base, w_block):
    n_rows = x.shape[0]
    tiles_per_mod = rows_per_mod // tm
    return pl.pallas_call(
        functools.partial(_proj_kernel, kind=kind),
        out_shape=jax.ShapeDtypeStruct((n_rows, n_cols), BF16),
        grid=(n_rows // tm, n_cols // tn),
        in_specs=[
            pl.BlockSpec((tm, D_MODEL), lambda i, j: (i, 0)),
            pl.BlockSpec((None, N_MOD, D_MODEL), lambda i, j: (mod_base + i // tiles_per_mod, 0, 0)),
            pl.BlockSpec((D_MODEL, tn), lambda i, j: (0, w_block(j))),
            pl.BlockSpec((1, tn), lambda i, j: (0, j)),
        ],
        out_specs=pl.BlockSpec((tm, tn), lambda i, j: (i, j)),
        scratch_shapes=[pltpu.VMEM((tm, D_MODEL), BF16)],
        compiler_params=_params("parallel", "arbitrary"),
        name="mix_proj",
    )(x, mod, w, scale)


def _rope_proj_kernel(x_ref, mod_ref, wk_ref, wq_ref, cos_ref, sin_ref, o_ref):
    h = _mix_modulate(x_ref, mod_ref)
    cos = cos_ref[...]
    sin = sin_ref[...]
    lane = lax.broadcasted_iota(jnp.int32, (1, LANES), 1)
    first = (lane & 32) == 0
    for w_ref, base, scale in ((wk_ref, ROPE_KR, RET_QK_DIM ** -0.5), (wq_ref, ROPE_QR, None)):
        acc = _dot(h, w_ref[...])
        for hd in range(RET_HEADS):
            xh = acc[:, hd * LANES:(hd + 1) * LANES]
            r = xh * cos + _swap_pairs(xh, first) * sin
            if scale is not None:
                r = r * scale
            o_ref[:, base + hd * LANES:base + (hd + 1) * LANES] = r.astype(BF16)


def _rope_proj(x, mod, w, cos, sin):
    tm = PROJ_ROWS
    tiles_per_batch = SEQ // tm
    w_cols = lambda off: pl.BlockSpec((D_MODEL, RET_QK_W), lambda i: (0, off // RET_QK_W),
                                      pipeline_mode=pl.Buffered(1))
    return pl.pallas_call(
        _rope_proj_kernel,
        out_shape=jax.ShapeDtypeStruct((N_LAT, ROPE_COLS), BF16),
        grid=(N_LAT // tm,),
        in_specs=[
            pl.BlockSpec((tm, D_MODEL), lambda i: (i, 0)),
            pl.BlockSpec((None, N_MOD, D_MODEL), lambda i: (i // tiles_per_batch, 0, 0)),
            w_cols(OFF_KR),
            w_cols(OFF_QR),
            pl.BlockSpec((tm, LANES), lambda i: (i % tiles_per_batch, 0)),
            pl.BlockSpec((tm, LANES), lambda i: (i % tiles_per_batch, 0)),
        ],
        out_specs=pl.BlockSpec((tm, ROPE_COLS), lambda i: (i, 0)),
        compiler_params=_params("parallel"),
        name="rope_proj",
    )(x, mod, w, w, cos, sin)


def _na_kernel(q_ref, k_ref, v_ref, kc_ref, vc_ref, bias_ref, o_ref, s0_ref, c0_ref, s1_ref, c1_ref):
    lane = lax.broadcasted_iota(jnp.int32, (1, LANES), 1)
    first_head = lane < NA_HEAD_DIM
    zero = jnp.zeros((), BF16)
    kc = kc_ref[...]
    vc = vc_ref[...]
    win = NA_WIN_ROWS * GRID_W
    rows2 = 2 * GRID_W
    n_groups = GRID_ROWS // NA_GROUP
    slots = ((s0_ref, c0_ref), (s1_ref, c1_ref))

    def window_start(r):
        return jnp.clip(r - NA_WIN_ROWS // 2, 0, GRID_ROWS - NA_WIN_ROWS)

    def tokens(r, n):
        return pl.ds(pl.multiple_of(r * GRID_W, GRID_W), n)

    def scores(t, slot):
        s_ref, c_ref = slots[slot]
        q2s = []
        for u in range(NA_GROUP):
            r = t * NA_GROUP + u
            q = q_ref[tokens(r, GRID_W), :]
            q2 = jnp.concatenate([jnp.where(first_head, q, zero), jnp.where(first_head, zero, q)], axis=0)
            s_ref[u] = _dot_nt(q2, k_ref[tokens(window_start(r), win), :])
            q2s.append(q2)
        c_ref[...] = _dot_nt(jnp.concatenate(q2s, axis=0), kc)

    def attend(t, slot):
        s_ref, c_ref = slots[slot]
        o_lat, p_ctx, den = [], [], []
        for u in range(NA_GROUP):
            r = t * NA_GROUP + u
            rs = window_start(r)
            delta = r - rs
            bias = jnp.concatenate(
                [bias_ref[2 * m - delta + NA_WIN_ROWS - 1] for m in range(NA_WIN_ROWS // 2)], axis=1)
            s_lat = s_ref[u] + bias
            s_ctx = c_ref[u * rows2:(u + 1) * rows2, :]
            mx = jnp.maximum(jnp.max(s_lat, axis=-1, keepdims=True), jnp.max(s_ctx, axis=-1, keepdims=True))
            e_lat = jnp.exp(s_lat - mx)
            e_ctx = jnp.exp(s_ctx - mx)
            den.append(jnp.sum(e_lat, axis=-1, keepdims=True) + jnp.sum(e_ctx, axis=-1, keepdims=True))
            o_lat.append(_dot(e_lat.astype(BF16), v_ref[tokens(rs, win), :]))
            p_ctx.append(e_ctx.astype(BF16))
        o_ctx = _dot(jnp.concatenate(p_ctx, axis=0), vc)
        for u in range(NA_GROUP):
            o2 = (o_lat[u] + o_ctx[u * rows2:(u + 1) * rows2]) / den[u]
            out = jnp.where(first_head, o2[:GRID_W], o2[GRID_W:])
            o_ref[tokens(t * NA_GROUP + u, GRID_W), :] = out.astype(BF16)

    scores(0, 0)

    def two_groups(i, carry):
        scores(2 * i + 1, 1)
        attend(2 * i, 0)
        scores(2 * i + 2, 0)
        attend(2 * i + 1, 1)
        return carry

    lax.fori_loop(0, n_groups // 2 - 1, two_groups, 0)
    scores(n_groups - 1, 1)
    attend(n_groups - 2, 0)
    attend(n_groups - 1, 1)


def _na(proj, cproj, bias):
    n_pairs = NA_HEADS // 2
    blk = lambda off: off // LANES
    return pl.pallas_call(
        _na_kernel,
        out_shape=jax.ShapeDtypeStruct((N_LAT, NA_W), BF16),
        grid=(BATCH, n_pairs),
        in_specs=[
            pl.BlockSpec((SEQ, LANES), lambda b, p: (b, blk(PLAIN_QA) + p)),
            pl.BlockSpec((SEQ, LANES), lambda b, p: (b, blk(PLAIN_KA) + p)),
            pl.BlockSpec((SEQ, LANES), lambda b, p: (b, blk(PLAIN_VA) + p)),
            pl.BlockSpec((CTX_LEN, LANES), lambda b, p: (b, blk(OFF_KA) + p)),
            pl.BlockSpec((CTX_LEN, LANES), lambda b, p: (b, blk(OFF_VA) + p)),
            pl.BlockSpec((None, 2 * NA_WIN_ROWS - 2, 2 * GRID_W, 2 * GRID_W), lambda b, p: (p, 0, 0, 0)),
        ],
        out_specs=pl.BlockSpec((SEQ, LANES), lambda b, p: (b, p)),
        scratch_shapes=2 * [pltpu.VMEM((NA_GROUP, 2 * GRID_W, NA_WIN_ROWS * GRID_W), F32),
                            pltpu.VMEM((NA_GROUP * 2 * GRID_W, CTX_LEN), F32)],
        compiler_params=_params("parallel", "parallel"),
        name="nbr_attn",
    )(proj, proj, proj, cproj, cproj, bias)


def _ret_kernel(lg_ref, q_ref, k_ref, v_ref, g_ref, kc_ref, vc_ref, o_ref, sb_ref, kvf_ref, a0_ref, a1_ref):
    h = pl.program_id(1)
    lgf = lg_ref[0, h]
    lgb = lg_ref[1, h]
    c = RET_CHUNK
    ri = lax.broadcasted_iota(jnp.int32, (c, c), 0).astype(F32)
    ci = lax.broadcasted_iota(jnp.int32, (c, c), 1).astype(F32)
    diff = ri - ci
    decay = (jnp.where(diff >= 0, jnp.exp(lgf * jnp.maximum(diff, 0.0)), 0.0)
             + jnp.where(diff < 0, jnp.exp(lgb * jnp.maximum(-diff, 0.0)), 0.0))
    zeta_f = jnp.exp(lgf * (c - 1 - ci))
    zeta_b = jnp.exp(lgb * ci)
    rv = lax.broadcasted_iota(jnp.int32, (c, RET_V_DIM), 0).astype(F32)
    xi_f = jnp.exp(lgf * (rv + 1.0))
    xi_b = jnp.exp(lgb * (c - rv))
    g_f = jnp.exp(jnp.full((c, RET_V_DIM), lgf * c, F32))
    g_b = jnp.exp(jnp.full((c, RET_V_DIM), lgb * c, F32))

    pc = lax.broadcasted_iota(jnp.int32, (CTX_LEN, RET_QK_DIM), 0).astype(F32)
    kc = kc_ref[...].astype(F32)
    vc = vc_ref[...]
    s0_f = _dot_tn((kc * jnp.exp(lgf * (CTX_LEN - 1 - pc))).astype(BF16), vc)
    s0_b = _dot_tn((kc * jnp.exp(lgb * pc)).astype(BF16), vc)

    def chunk(n):
        return pl.ds(pl.multiple_of(n * c, c), c)

    def bwd_one(n, s):
        sb_ref[n] = s.astype(BF16)
        kt = k_ref[chunk(n), :].astype(F32).T
        v = v_ref[chunk(n), :]
        kvf_ref[n] = _dot((kt * zeta_f).astype(BF16), v)
        return g_b * s + _dot((kt * zeta_b).astype(BF16), v)

    def bwd(t, s):
        for u in range(RET_UNROLL):
            s = bwd_one(N_CHUNKS - 1 - (t * RET_UNROLL + u), s)
        return s

    lax.fori_loop(0, N_CHUNKS // RET_UNROLL, bwd, s0_b)

    n_groups = N_CHUNKS // RET_GROUP

    def fwd_scores(t, a_ref):
        for u in range(RET_GROUP):
            n = t * RET_GROUP + u
            a_ref[u] = _dot_nt(q_ref[chunk(n), :], k_ref[chunk(n), :])

    def fwd_out(t, a_ref, s):
        for u in range(RET_GROUP):
            n = t * RET_GROUP + u
            q = q_ref[chunk(n), :]
            scores = (a_ref[u] * decay).astype(BF16)
            cross = _dot(q, jnp.concatenate([s.astype(BF16), sb_ref[n]], axis=1))
            o = (_dot(scores, v_ref[chunk(n), :])
                 + cross[:, :RET_V_DIM] * xi_f
                 + cross[:, RET_V_DIM:] * xi_b)
            mu = jnp.mean(o, axis=-1, keepdims=True)
            oc = o - mu
            var = jnp.mean(oc * oc, axis=-1, keepdims=True)
            gn = oc * lax.rsqrt(var + GN_EPS)
            o_ref[chunk(n), :] = (g_ref[chunk(n), :].astype(F32) * gn).astype(BF16)
            s = g_f * s + kvf_ref[n]
        return s

    fwd_scores(0, a0_ref)

    def two_groups(i, s):
        fwd_scores(2 * i + 1, a1_ref)
        s = fwd_out(2 * i, a0_ref, s)
        fwd_scores(2 * i + 2, a0_ref)
        return fwd_out(2 * i + 1, a1_ref, s)

    s = lax.fori_loop(0, n_groups // 2 - 1, two_groups, s0_f)
    fwd_scores(n_groups - 1, a1_ref)
    s = fwd_out(n_groups - 2, a0_ref, s)
    fwd_out(n_groups - 1, a1_ref, s)


def _ret(log_gamma, rproj, proj, gr, cproj):
    qk = lambda off: off // RET_QK_DIM
    vv = lambda off: off // RET_V_DIM
    return pl.pallas_call(
        _ret_kernel,
        out_shape=jax.ShapeDtypeStruct((N_LAT, RET_V_W), BF16),
        grid=(BATCH, RET_HEADS),
        in_specs=[
            pl.BlockSpec(memory_space=pltpu.SMEM),
            pl.BlockSpec((SEQ, RET_QK_DIM), lambda b, h: (b, qk(ROPE_QR) + h)),
            pl.BlockSpec((SEQ, RET_QK_DIM), lambda b, h: (b, qk(ROPE_KR) + h)),
            pl.BlockSpec((SEQ, RET_V_DIM), lambda b, h: (b, vv(PLAIN_VR) + h)),
            pl.BlockSpec((SEQ, RET_V_DIM), lambda b, h: (b, h)),
            pl.BlockSpec((CTX_LEN, RET_QK_DIM), lambda b, h: (b, qk(OFF_KR) + h)),
            pl.BlockSpec((CTX_LEN, RET_V_DIM), lambda b, h: (b, vv(OFF_VR) + h)),
        ],
        out_specs=pl.BlockSpec((SEQ, RET_V_DIM), lambda b, h: (b, h)),
        scratch_shapes=[pltpu.VMEM((N_CHUNKS, RET_QK_DIM, RET_V_DIM), BF16),
                        pltpu.VMEM((N_CHUNKS, RET_QK_DIM, RET_V_DIM), F32),
                        pltpu.VMEM((RET_GROUP, RET_CHUNK, RET_CHUNK), F32),
                        pltpu.VMEM((RET_GROUP, RET_CHUNK, RET_CHUNK), F32)],
        compiler_params=_params("parallel", "parallel"),
        name="retention",
    )(log_gamma, rproj, rproj, proj, gr, cproj, cproj)


def _merge_kernel(ona_ref, ret_ref, gna_ref, gret_ref, x_ref, mod_ref,
                  wna_ref, wret_ref, wout_ref, lng_ref, lnb_ref, *rest, n_cast):
    o_ref = rest[n_cast]
    _cast_blocks(rest[:n_cast], rest[n_cast + 1:])
    y_na = _dot(ona_ref[...], wna_ref[...])
    y_ret = _dot(ret_ref[...], wret_ref[...])
    z = (gna_ref[...].astype(F32) * y_na + gret_ref[...].astype(F32) * y_ret).astype(BF16)
    y = _dot(z, wout_ref[...])
    g = mod_ref[5:6, :]
    o_ref[...] = _layer_norm(ALPHA * x_ref[...] + g * y, lng_ref[...], lnb_ref[...], LN_EPS)


def _merge(o_na, ret, gates, x, mod, w_na, w_ret, w_out, ln_g, ln_b, *, cast=()):
    tm = MERGE_ROWS
    tiles_per_batch = SEQ // tm
    n_steps = N_LAT // tm
    resident = lambda shape: pl.BlockSpec(shape, lambda i: (0, 0), pipeline_mode=pl.Buffered(1))
    cast = tuple(cast)
    cast_spec = lambda wr: _cast_spec(wr[0], wr[1], lambda i: i, n_steps)
    out = pl.pallas_call(
        functools.partial(_merge_kernel, n_cast=len(cast)),
        out_shape=[jax.ShapeDtypeStruct((N_LAT, D_MODEL), F32)]
        + [jax.ShapeDtypeStruct(w.shape, BF16) for w, _ in cast],
        grid=(n_steps,),
        in_specs=[
            pl.BlockSpec((tm, NA_W), lambda i: (i, 0)),
            pl.BlockSpec((tm, RET_V_W), lambda i: (i, 0)),
            pl.BlockSpec((tm, D_MODEL), lambda i: (i, 0)),
            pl.BlockSpec((tm, D_MODEL), lambda i: (i, 1)),
            pl.BlockSpec((tm, D_MODEL), lambda i: (i, 0)),
            pl.BlockSpec((None, N_MOD, D_MODEL), lambda i: (i // tiles_per_batch, 0, 0)),
            resident((NA_W, D_MODEL)),
            resident((RET_V_W, D_MODEL)),
            resident((D_MODEL, D_MODEL)),
            pl.BlockSpec((1, D_MODEL), lambda i: (0, 0)),
            pl.BlockSpec((1, D_MODEL), lambda i: (0, 0)),
        ] + [cast_spec(wr) for wr in cast],
        out_specs=[pl.BlockSpec((tm, D_MODEL), lambda i: (i, 0))] + [cast_spec(wr) for wr in cast],
        compiler_params=_params("arbitrary"),
        name="merge",
    )(o_na, ret, gates, gates, x, mod, w_na, w_ret, w_out, ln_g, ln_b, *[w for w, _ in cast])
    return out if cast else out[0]


def _rope_tables():
    nfreq = RET_QK_DIM // 4
    pos = jnp.arange(SEQ)
    prow = (pos // GRID_W).astype(F32)
    pcol = (pos % GRID_W).astype(F32)
    freqs = ROPE_BASE ** (-jnp.arange(nfreq, dtype=F32) / nfreq)
    ang = jnp.stack([prow, pcol], axis=-1)[:, :, None] * freqs
    cos = jnp.cos(ang)
    sin = jnp.sin(ang)
    cos_t = jnp.stack([cos, cos], axis=2).reshape(SEQ, RET_QK_DIM)
    sin_t = jnp.stack([-sin, sin], axis=2).reshape(SEQ, RET_QK_DIM)
    return cos_t, sin_t


def _na_bias_table(rpb):
    wc = NA_WIN_COLS
    n_dc = 2 * wc - 1
    col = jnp.arange(GRID_W)
    col_start = jnp.clip(col - wc // 2, 0, GRID_W - wc)
    col_mask = (col[None, :] >= col_start[:, None]) & (col[None, :] < col_start[:, None] + wc)
    dc_idx = jnp.clip(col[None, :] - col[:, None] + wc - 1, 0, n_dc - 1)
    onehot = ((dc_idx[None] == jnp.arange(n_dc)[:, None, None]) & col_mask[None]).astype(F32)
    t = jnp.einsum('hdc,cqk->hdqk', rpb, onehot, precision=lax.Precision.HIGHEST)
    t = t + jnp.where(col_mask, 0.0, NEG_INF)
    n_dr = 2 * NA_WIN_ROWS - 1
    t = t.reshape(NA_HEADS // 2, 2, n_dr, GRID_W, GRID_W).transpose(0, 2, 1, 3, 4)
    t = t.reshape(NA_HEADS // 2, n_dr, 2 * GRID_W, GRID_W)
    return jnp.concatenate([t[:, :-1], t[:, 1:]], axis=-1)


def _column_scale(n_cols, lo, hi, value):
    col = jnp.arange(n_cols)
    return jnp.where((col >= lo) & (col < hi), value, 1.0).astype(F32).reshape(1, n_cols)


def kernel(x, c, ctx, c_ctx, w_ada, b_ada, ln_g, ln_b, ffn1_w_in, ffn1_w_out, ffn2_w_in, ffn2_w_out,
           w_mix_in, na_rpb, ret_decay_fwd, ret_decay_bwd, w_na_out, w_ret_out, w_mix_out):
    assert x.shape == (BATCH, SEQ, D_MODEL) and ctx.shape == (BATCH, CTX_LEN, D_MODEL)
    assert w_ada.shape == (DEPTH, D_MODEL, N_MOD * D_MODEL) and w_mix_in.shape == (DEPTH, D_MODEL, MIX_COLS)
    l = 0
    ctx_mod = BATCH
    cc = jnp.zeros((MOD_ROWS, D_MODEL), F32).at[:BATCH].set(c).at[ctx_mod].set(c_ctx)
    mod = _adaln(cc, w_ada[l], b_ada[l].reshape(1, -1)).reshape(MOD_ROWS, N_MOD, D_MODEL)

    row = lambda v: v.reshape(1, D_MODEL)
    w1_in, w1_out = ffn1_w_in[l].astype(BF16), ffn1_w_out[l].astype(BF16)

    xf = x.reshape(N_LAT, D_MODEL)
    cf = ctx.reshape(N_CTX, D_MODEL)
    w_na, w_ret, w_out = w_na_out[l].astype(BF16), w_ret_out[l].astype(BF16), w_mix_out[l].astype(BF16)
    x1, w_mix = _ffn(xf, mod, w1_in, w1_out, row(ln_g[l, 0]), row(ln_b[l, 0]), k0=0, tm=FFN_ROWS,
                     rows_per_mod=SEQ, mod_base=0, cast=[(w_mix_in[l], CAST_ROWS_FFN)])
    c1 = _ffn(cf, mod, w1_in, w1_out, row(ln_g[l, 0]), row(ln_b[l, 0]), k0=0, tm=N_CTX,
              rows_per_mod=N_CTX, mod_base=ctx_mod)

    cos_t, sin_t = _rope_tables()
    rproj = _rope_proj(x1, mod, w_mix, cos_t, sin_t)
    latent = dict(tm=PROJ_ROWS, tn=PROJ_COLS, rows_per_mod=SEQ, mod_base=0)
    ones = jnp.ones((1, GATE_COLS), F32)
    proj = _proj(x1, mod, w_mix, _column_scale(PLAIN_COLS, PLAIN_QA, PLAIN_COLS, NA_HEAD_DIM ** -0.5),
                 kind=KIND_SCALE, n_cols=PLAIN_COLS,
                 w_block=lambda j: j + (j >= OFF_KR // PROJ_COLS).astype(jnp.int32), **latent)
    gr = _proj(x1, mod, w_mix, ones, kind=KIND_SILU, n_cols=RET_V_W,
               w_block=lambda j: OFF_GR // PROJ_COLS + j, **latent)
    gates = _proj(x1, mod, w_mix, ones, kind=KIND_SIGMOID, n_cols=GATE_COLS,
                  w_block=lambda j: OFF_GNA // PROJ_COLS + j, **latent)
    cproj = _proj(c1, mod, w_mix, _column_scale(KV_COLS, OFF_KR, OFF_VR, RET_QK_DIM ** -0.5),
                  kind=KIND_SCALE, n_cols=KV_COLS, tm=N_CTX, tn=CTX_PROJ_COLS,
                  rows_per_mod=N_CTX, mod_base=ctx_mod, w_block=lambda j: j)

    o_na = _na(proj, cproj, _na_bias_table(na_rpb[l]))
    log_gamma = jnp.stack([jax.nn.log_sigmoid(ret_decay_fwd[l].astype(F32)),
                           jax.nn.log_sigmoid(ret_decay_bwd[l].astype(F32))])
    ret = _ret(log_gamma, rproj, proj, gr, cproj)

    x2, w2_in, w2_out = _merge(o_na, ret, gates, x1, mod, w_na, w_ret, w_out, row(ln_g[l, 1]), row(ln_b[l, 1]),
                               cast=[(ffn2_w_in[l], CAST_ROWS_MERGE_IN), (ffn2_w_out[l], CAST_ROWS_MERGE_OUT)])
    x3 = _ffn(x2, mod, w2_in, w2_out, row(ln_g[l, 2]), row(ln_b[l, 2]), k0=6, tm=FFN_ROWS,
              rows_per_mod=SEQ, mod_base=0)
    return x3.reshape(BATCH, SEQ, D_MODEL)
```

```python
import functools
from typing import NamedTuple

import jax
import jax.numpy as jnp
from jax import lax
from jax.experimental import pallas as pl
from jax.experimental.pallas import tpu as pltpu

D_MODEL = 2048
BATCH = 2
SEQ = 8192
GRID_W = 64
CTX_LEN = 256
NA_HEADS = 16
NA_HEAD_DIM = 64
NA_W = NA_HEADS * NA_HEAD_DIM
NA_WIN_ROWS = 8
NA_WIN_COLS = 16
RET_HEADS = 8
RET_QK_DIM = 128
RET_V_DIM = 256
RET_QK_W = RET_HEADS * RET_QK_DIM
RET_V_W = RET_HEADS * RET_V_DIM
RET_CHUNK = 128
D_FF = 5632
N_MOD = 9
ROPE_BASE = 10000.0
LN_EPS = 1e-5
GN_EPS = 1e-6
NEG_INF = -1e30
DEPTH = 1
ALPHA = (2 * DEPTH) ** 0.25
MIX_COLS = 2 * NA_W + RET_QK_W + RET_V_W + NA_W + RET_QK_W + RET_V_W + 2 * D_MODEL
KV_COLS = 2 * NA_W + RET_QK_W + RET_V_W

OFF_KA = 0
OFF_VA = NA_W
OFF_KR = 2 * NA_W
OFF_VR = OFF_KR + RET_QK_W
OFF_QA = OFF_VR + RET_V_W
OFF_QR = OFF_QA + NA_W
OFF_GR = OFF_QR + RET_QK_W
OFF_GNA = OFF_GR + RET_V_W
OFF_GRET = OFF_GNA + D_MODEL

ROPE_COLS = 2 * RET_QK_W
ROPE_KR = 0
ROPE_QR = RET_QK_W
PLAIN_KA = 0
PLAIN_VA = PLAIN_KA + NA_W
PLAIN_VR = PLAIN_VA + NA_W
PLAIN_QA = PLAIN_VR + RET_V_W
PLAIN_COLS = PLAIN_QA + NA_W
GATE_COLS = 2 * D_MODEL

KIND_SCALE = 0
KIND_SILU = 1
KIND_SIGMOID = 2

N_LAT = BATCH * SEQ
N_CTX = BATCH * CTX_LEN
GRID_ROWS = SEQ // GRID_W
N_CHUNKS = SEQ // RET_CHUNK

V7X_VMEM_BYTES = 64 * 1024 * 1024
VMEM_LIMIT_BYTES = V7X_VMEM_BYTES - 8 * 1024 * 1024
LANES = 128

FFN_ROWS = 512
FFN_COLS = 512
FFN_SUB = 256
N_SUB = D_FF // FFN_SUB
PROJ_ROWS = 1024
PROJ_COLS = 1024
CTX_PROJ_COLS = 2560
MERGE_ROWS = 256
ADA_COLS = 1024
MOD_ROWS = 8
NA_GROUP = 2
RET_GROUP = 4
RET_UNROLL = 8
CAST_ROWS_FFN = 16
CAST_ROWS_MERGE_IN = 32
CAST_ROWS_MERGE_OUT = 176

BF16 = jnp.bfloat16
F32 = jnp.float32


def _dot(a, b):
    return jnp.dot(a, b, preferred_element_type=F32)


def _dot_nt(a, b):
    return lax.dot_general(a, b, (((1,), (1,)), ((), ())), preferred_element_type=F32)


def _dot_tn(a, b):
    return lax.dot_general(a, b, (((0,), (0,)), ((), ())), preferred_element_type=F32)


def _sigmoid(x):
    return 0.5 * jnp.tanh(0.5 * x) + 0.5


def _layer_norm(y, g, b, eps):
    mu = jnp.mean(y, axis=-1, keepdims=True)
    yc = y - mu
    var = jnp.mean(yc * yc, axis=-1, keepdims=True)
    return yc * lax.rsqrt(var + eps) * g + b


def _params(*semantics):
    return pltpu.CompilerParams(dimension_semantics=semantics, vmem_limit_bytes=VMEM_LIMIT_BYTES)


def _swiglu_layout_shape(w):
    return (N_SUB, w.shape[0], 2 * FFN_SUB)


class _CastJob(NamedTuple):
    w: jax.Array
    rows: int
    swiglu: bool = False


def _cast_out_shape(job):
    shape = _swiglu_layout_shape(job.w) if job.swiglu else job.w.shape
    return jax.ShapeDtypeStruct(shape, BF16)


def _cast_specs(job, step, n_steps):
    n_rows, n_cols = job.w.shape
    n_blocks = n_rows // job.rows
    assert n_rows % job.rows == 0 and job.rows % 16 == 0 and n_blocks <= n_steps
    block = lambda *g: jnp.minimum(step(*g), n_blocks - 1)
    src = pl.BlockSpec((job.rows, n_cols), lambda *g: (block(*g), 0))
    if job.swiglu:
        return src, pl.BlockSpec((N_SUB, job.rows, 2 * FFN_SUB), lambda *g: (0, block(*g), 0))
    return src, src


def _cast_blocks(src_refs, dst_refs):
    for src, dst in zip(src_refs, dst_refs):
        w = src[...].astype(BF16)
        if len(dst.shape) == 2:
            dst[...] = w
        else:
            for k in range(N_SUB):
                dst[k, :, :FFN_SUB] = w[:, k * FFN_SUB:(k + 1) * FFN_SUB]
                dst[k, :, FFN_SUB:] = w[:, D_FF + k * FFN_SUB:D_FF + (k + 1) * FFN_SUB]


def _adaln_kernel(c_ref, w_ref, b_ref, o_ref):
    cc = c_ref[...]
    s = (cc * _sigmoid(cc)).astype(BF16)
    o_ref[...] = _dot(s, w_ref[...].astype(BF16)) + b_ref[...]


def _adaln(cc, w_ada, b_ada):
    n_out = N_MOD * D_MODEL
    return pl.pallas_call(
        _adaln_kernel,
        out_shape=jax.ShapeDtypeStruct((MOD_ROWS, n_out), F32),
        grid=(n_out // ADA_COLS,),
        in_specs=[
            pl.BlockSpec((MOD_ROWS, D_MODEL), lambda j: (0, 0)),
            pl.BlockSpec((D_MODEL, ADA_COLS), lambda j: (0, j)),
            pl.BlockSpec((1, ADA_COLS), lambda j: (0, j)),
        ],
        out_specs=pl.BlockSpec((MOD_ROWS, ADA_COLS), lambda j: (0, j)),
        compiler_params=_params("parallel"),
        name="adaln",
    )(cc, w_ada, b_ada)


def _ffn_first_step(x_ref, mod_ref, h_ref, o_ref, k0):
    shift = mod_ref[k0:k0 + 1, :]
    scale = mod_ref[k0 + 1:k0 + 2, :]
    h_ref[...] = (x_ref[...] * (1.0 + scale) + shift).astype(BF16)
    o_ref[...] = jnp.zeros_like(o_ref)


def _ffn_last_step(x_ref, mod_ref, lng_ref, lnb_ref, o_ref, mix_ref, k0):
    g = mod_ref[k0 + 2:k0 + 3, :]
    y = ALPHA * x_ref[...] + g * (0.5 * o_ref[...])
    o_ref[...] = _layer_norm(y, lng_ref[...], lnb_ref[...], LN_EPS)
    if mix_ref is not None:
        mix_ref[...] = (o_ref[...] * (1.0 + mod_ref[4:5, :]) + mod_ref[3:4, :]).astype(BF16)


def _swiglu(gate_up):
    gate = gate_up[:, :FFN_SUB]
    up = gate_up[:, FFN_SUB:]
    return (gate * _sigmoid(gate) * up).astype(BF16)


def _ffn_kernel(x_ref, mod_ref, wgu_ref, wo_ref, lng_ref, lnb_ref, *rest, k0, n_cast, mix_out):
    cast_src = rest[:n_cast]
    outs = rest[n_cast:]
    o_ref = outs[0]
    mix_ref = outs[1] if mix_out else None
    cast_dst = outs[1 + mix_out:1 + mix_out + n_cast]
    h_ref = outs[1 + mix_out + n_cast]
    j = pl.program_id(1)

    @pl.when(j == 0)
    def _():
        _ffn_first_step(x_ref, mod_ref, h_ref, o_ref, k0)

    h = h_ref[...]
    act = jnp.concatenate([_swiglu(_dot(h, wgu_ref[k])) for k in range(wgu_ref.shape[0])], axis=1)
    o_ref[...] += _dot(act, wo_ref[...])
    _cast_blocks(cast_src, cast_dst)

    @pl.when(j == pl.num_programs(1) - 1)
    def _():
        _ffn_last_step(x_ref, mod_ref, lng_ref, lnb_ref, o_ref, mix_ref, k0)


def _ffn(x, mod, w_in, w_out, ln_g, ln_b, *, k0, tm, rows_per_mod, mod_base, mix_out=False, cast=()):
    n_rows = x.shape[0]
    row_block = pl.BlockSpec((tm, D_MODEL), lambda i, j: (i, 0))
    tf = FFN_COLS
    tiles_per_mod = rows_per_mod // tm
    n_col = D_FF // tf
    n_steps = (n_rows // tm) * n_col
    cast = tuple(cast)
    cast_specs = [_cast_specs(job, lambda i, j: i * n_col + j, n_steps) for job in cast]

    mix_shape = [jax.ShapeDtypeStruct((n_rows, D_MODEL), BF16)] if mix_out else []
    out = pl.pallas_call(
        functools.partial(_ffn_kernel, k0=k0, n_cast=len(cast), mix_out=mix_out),
        out_shape=[jax.ShapeDtypeStruct((n_rows, D_MODEL), F32)] + mix_shape
        + [_cast_out_shape(job) for job in cast],
        grid=(n_rows // tm, n_col),
        in_specs=[
            row_block,
            pl.BlockSpec((None, N_MOD, D_MODEL), lambda i, j: (mod_base + i // tiles_per_mod, 0, 0)),
            pl.BlockSpec((tf // FFN_SUB, D_MODEL, 2 * FFN_SUB), lambda i, j: (j, 0, 0)),
            pl.BlockSpec((tf, D_MODEL), lambda i, j: (j, 0)),
            pl.BlockSpec((1, D_MODEL), lambda i, j: (0, 0)),
            pl.BlockSpec((1, D_MODEL), lambda i, j: (0, 0)),
        ] + [src for src, _ in cast_specs],
        out_specs=[row_block] + mix_out * [row_block] + [dst for _, dst in cast_specs],
        scratch_shapes=[pltpu.VMEM((tm, D_MODEL), BF16)],
        compiler_params=_params("arbitrary", "arbitrary"),
        name="ffn",
    )(x, mod, w_in, w_out, ln_g, ln_b, *[job.w for job in cast])
    return out if len(out) > 1 else out[0]


def _ffn_convert_kernel(x_ref, mod_ref, wg_ref, wu_ref, wo_ref, lng_ref, lnb_ref,
                        o_ref, mix_ref, wgu_out_ref, wo_out_ref, h_ref, *, k0):
    j = pl.program_id(1)

    @pl.when(j == 0)
    def _():
        _ffn_first_step(x_ref, mod_ref, h_ref, o_ref, k0)

    wgu = jnp.concatenate([wg_ref[...].astype(BF16), wu_ref[...].astype(BF16)], axis=1)
    wo = wo_ref[...].astype(BF16)
    wgu_out_ref[...] = wgu
    wo_out_ref[...] = wo
    o_ref[...] += _dot(_swiglu(_dot(h_ref[...], wgu)), wo)

    @pl.when(j == pl.num_programs(1) - 1)
    def _():
        _ffn_last_step(x_ref, mod_ref, lng_ref, lnb_ref, o_ref, mix_ref, k0)


def _ffn_convert(x, mod, w_in, w_out, ln_g, ln_b, *, k0, mod_base):
    n_rows = x.shape[0]
    return pl.pallas_call(
        functools.partial(_ffn_convert_kernel, k0=k0),
        out_shape=[jax.ShapeDtypeStruct((n_rows, D_MODEL), F32),
                   jax.ShapeDtypeStruct((n_rows, D_MODEL), BF16),
                   jax.ShapeDtypeStruct(_swiglu_layout_shape(w_in), BF16),
                   jax.ShapeDtypeStruct(w_out.shape, BF16)],
        grid=(1, N_SUB),
        in_specs=[
            pl.BlockSpec((n_rows, D_MODEL), lambda i, j: (0, 0)),
            pl.BlockSpec((None, N_MOD, D_MODEL), lambda i, j: (mod_base, 0, 0)),
            pl.BlockSpec((D_MODEL, FFN_SUB), lambda i, j: (0, j)),
            pl.BlockSpec((D_MODEL, FFN_SUB), lambda i, j: (0, j + N_SUB)),
            pl.BlockSpec((FFN_SUB, D_MODEL), lambda i, j: (j, 0)),
            pl.BlockSpec((1, D_MODEL), lambda i, j: (0, 0)),
            pl.BlockSpec((1, D_MODEL), lambda i, j: (0, 0)),
        ],
        out_specs=[pl.BlockSpec((n_rows, D_MODEL), lambda i, j: (0, 0)),
                   pl.BlockSpec((n_rows, D_MODEL), lambda i, j: (0, 0)),
                   pl.BlockSpec((None, D_MODEL, 2 * FFN_SUB), lambda i, j: (j, 0, 0)),
                   pl.BlockSpec((FFN_SUB, D_MODEL), lambda i, j: (j, 0))],
        scratch_shapes=[pltpu.VMEM((n_rows, D_MODEL), BF16)],
        compiler_params=_params("arbitrary", "arbitrary"),
        name="ffn_convert",
    )(x, mod, w_in, w_in, w_out, ln_g, ln_b)


def _swap_pairs(x, lane_is_first):
    return jnp.where(lane_is_first, pltpu.roll(x, LANES - 32, 1), pltpu.roll(x, 32, 1))


def _proj_kernel(h_ref, w_ref, scale_ref, o_ref, *, kind):
    acc = _dot(h_ref[...], w_ref[...])
    if kind == KIND_SCALE:
        out = acc * scale_ref[...]
    elif kind == KIND_SILU:
        out = acc * _sigmoid(acc)
    else:
        out = _sigmoid(acc)
    o_ref[...] = out.astype(BF16)


def _proj(h, w, scale, *, kind, n_cols, tm, tn, w_block):
    n_rows = h.shape[0]
    return pl.pallas_call(
        functools.partial(_proj_kernel, kind=kind),
        out_shape=jax.ShapeDtypeStruct((n_rows, n_cols), BF16),
        grid=(n_rows // tm, n_cols // tn),
        in_specs=[
            pl.BlockSpec((tm, D_MODEL), lambda i, j: (i, 0)),
            pl.BlockSpec((D_MODEL, tn), lambda i, j: (0, w_block(j))),
            pl.BlockSpec((1, tn), lambda i, j: (0, j)),
        ],
        out_specs=pl.BlockSpec((tm, tn), lambda i, j: (i, j)),
        compiler_params=_params("parallel", "parallel"),
        name="mix_proj",
    )(h, w, scale)


def _rope_proj_kernel(h_ref, wk_ref, wq_ref, cos_ref, sin_ref, o_ref):
    h = h_ref[...]
    cos = cos_ref[...]
    sin = sin_ref[...]
    lane = lax.broadcasted_iota(jnp.int32, (1, LANES), 1)
    first = (lane & 32) == 0
    for w_ref, base, scale in ((wk_ref, ROPE_KR, RET_QK_DIM ** -0.5), (wq_ref, ROPE_QR, None)):
        acc = _dot(h, w_ref[...])
        for hd in range(RET_HEADS):
            xh = acc[:, hd * LANES:(hd + 1) * LANES]
            r = xh * cos + _swap_pairs(xh, first) * sin
            if scale is not None:
                r = r * scale
            o_ref[:, base + hd * LANES:base + (hd + 1) * LANES] = r.astype(BF16)


def _rope_proj(h, w, cos, sin):
    tm = PROJ_ROWS
    tiles_per_batch = SEQ // tm
    w_cols = lambda off: pl.BlockSpec((D_MODEL, RET_QK_W), lambda i: (0, off // RET_QK_W),
                                      pipeline_mode=pl.Buffered(1))
    return pl.pallas_call(
        _rope_proj_kernel,
        out_shape=jax.ShapeDtypeStruct((N_LAT, ROPE_COLS), BF16),
        grid=(N_LAT // tm,),
        in_specs=[
            pl.BlockSpec((tm, D_MODEL), lambda i: (i, 0)),
            w_cols(OFF_KR),
            w_cols(OFF_QR),
            pl.BlockSpec((tm, LANES), lambda i: (i % tiles_per_batch, 0)),
            pl.BlockSpec((tm, LANES), lambda i: (i % tiles_per_batch, 0)),
        ],
        out_specs=pl.BlockSpec((tm, ROPE_COLS), lambda i: (i, 0)),
        compiler_params=_params("parallel"),
        name="rope_proj",
    )(h, w, w, cos, sin)


def _na_kernel(q_ref, k_ref, v_ref, kc_ref, vc_ref, bias_ref, o_ref, s0_ref, c0_ref, s1_ref, c1_ref):
    lane = lax.broadcasted_iota(jnp.int32, (1, LANES), 1)
    first_head = lane < NA_HEAD_DIM
    zero = jnp.zeros((), BF16)
    kc = kc_ref[...]
    vc = vc_ref[...]
    win = NA_WIN_ROWS * GRID_W
    rows2 = 2 * GRID_W
    n_groups = GRID_ROWS // NA_GROUP
    slots = ((s0_ref, c0_ref), (s1_ref, c1_ref))

    def window_start(r):
        return jnp.clip(r - NA_WIN_ROWS // 2, 0, GRID_ROWS - NA_WIN_ROWS)

    def tokens(r, n):
        return pl.ds(pl.multiple_of(r * GRID_W, GRID_W), n)

    def scores(t, slot):
        s_ref, c_ref = slots[slot]
        q2s = []
        for u in range(NA_GROUP):
            r = t * NA_GROUP + u
            q = q_ref[tokens(r, GRID_W), :]
            q2 = jnp.concatenate([jnp.where(first_head, q, zero), jnp.where(first_head, zero, q)], axis=0)
            s_ref[u] = _dot_nt(q2, k_ref[tokens(window_start(r), win), :])
            q2s.append(q2)
        c_ref[...] = _dot_nt(jnp.concatenate(q2s, axis=0), kc)

    def attend(t, slot):
        s_ref, c_ref = slots[slot]
        o_lat, p_ctx, den = [], [], []
        for u in range(NA_GROUP):
            r = t * NA_GROUP + u
            rs = window_start(r)
            delta = r - rs
            bias = jnp.concatenate(
                [bias_ref[2 * m - delta + NA_WIN_ROWS - 1] for m in range(NA_WIN_ROWS // 2)], axis=1)
            s_lat = s_ref[u] + bias
            s_ctx = c_ref[u * rows2:(u + 1) * rows2, :]
            mx = jnp.maximum(jnp.max(s_lat, axis=-1, keepdims=True), jnp.max(s_ctx, axis=-1, keepdims=True))
            e_lat = jnp.exp(s_lat - mx)
            e_ctx = jnp.exp(s_ctx - mx)
            den.append(jnp.sum(e_lat, axis=-1, keepdims=True) + jnp.sum(e_ctx, axis=-1, keepdims=True))
            o_lat.append(_dot(e_lat.astype(BF16), v_ref[tokens(rs, win), :]))
            p_ctx.append(e_ctx.astype(BF16))
        o_ctx = _dot(jnp.concatenate(p_ctx, axis=0), vc)
        for u in range(NA_GROUP):
            o2 = (o_lat[u] + o_ctx[u * rows2:(u + 1) * rows2]) / den[u]
            out = jnp.where(first_head, o2[:GRID_W], o2[GRID_W:])
            o_ref[tokens(t * NA_GROUP + u, GRID_W), :] = out.astype(BF16)

    scores(0, 0)

    def two_groups(i, carry):
        scores(2 * i + 1, 1)
        attend(2 * i, 0)
        scores(2 * i + 2, 0)
        attend(2 * i + 1, 1)
        return carry

    lax.fori_loop(0, n_groups // 2 - 1, two_groups, 0)
    scores(n_groups - 1, 1)
    attend(n_groups - 2, 0)
    attend(n_groups - 1, 1)


def _na(proj, cproj, bias):
    n_pairs = NA_HEADS // 2
    blk = lambda off: off // LANES
    return pl.pallas_call(
        _na_kernel,
        out_shape=jax.ShapeDtypeStruct((N_LAT, NA_W), BF16),
        grid=(BATCH, n_pairs),
        in_specs=[
            pl.BlockSpec((SEQ, LANES), lambda b, p: (b, blk(PLAIN_QA) + p)),
            pl.BlockSpec((SEQ, LANES), lambda b, p: (b, blk(PLAIN_KA) + p)),
            pl.BlockSpec((SEQ, LANES), lambda b, p: (b, blk(PLAIN_VA) + p)),
            pl.BlockSpec((CTX_LEN, LANES), lambda b, p: (b, blk(OFF_KA) + p)),
            pl.BlockSpec((CTX_LEN, LANES), lambda b, p: (b, blk(OFF_VA) + p)),
            pl.BlockSpec((None, 2 * NA_WIN_ROWS - 2, 2 * GRID_W, 2 * GRID_W), lambda b, p: (p, 0, 0, 0)),
        ],
        out_specs=pl.BlockSpec((SEQ, LANES), lambda b, p: (b, p)),
        scratch_shapes=2 * [pltpu.VMEM((NA_GROUP, 2 * GRID_W, NA_WIN_ROWS * GRID_W), F32),
                            pltpu.VMEM((NA_GROUP * 2 * GRID_W, CTX_LEN), F32)],
        compiler_params=_params("parallel", "parallel"),
        name="nbr_attn",
    )(proj, proj, proj, cproj, cproj, bias)


def _ret_kernel(lg_ref, q_ref, k_ref, v_ref, g_ref, kc_ref, vc_ref, o_ref, sb_ref, kvf_ref, a0_ref, a1_ref):
    h = pl.program_id(1)
    lgf = lg_ref[0, h]
    lgb = lg_ref[1, h]
    c = RET_CHUNK
    ri = lax.broadcasted_iota(jnp.int32, (c, c), 0).astype(F32)
    ci = lax.broadcasted_iota(jnp.int32, (c, c), 1).astype(F32)
    diff = ri - ci
    decay = (jnp.where(diff >= 0, jnp.exp(lgf * jnp.maximum(diff, 0.0)), 0.0)
             + jnp.where(diff < 0, jnp.exp(lgb * jnp.maximum(-diff, 0.0)), 0.0))
    zeta_f = jnp.exp(lgf * (c - 1 - ci))
    zeta_b = jnp.exp(lgb * ci)
    rv = lax.broadcasted_iota(jnp.int32, (c, RET_V_DIM), 0).astype(F32)
    xi_f = jnp.exp(lgf * (rv + 1.0))
    xi_b = jnp.exp(lgb * (c - rv))
    g_f = jnp.exp(jnp.full((c, RET_V_DIM), lgf * c, F32))
    g_b = jnp.exp(jnp.full((c, RET_V_DIM), lgb * c, F32))

    pc = lax.broadcasted_iota(jnp.int32, (CTX_LEN, RET_QK_DIM), 0).astype(F32)
    kc = kc_ref[...].astype(F32)
    vc = vc_ref[...]
    s0_f = _dot_tn((kc * jnp.exp(lgf * (CTX_LEN - 1 - pc))).astype(BF16), vc)
    s0_b = _dot_tn((kc * jnp.exp(lgb * pc)).astype(BF16), vc)

    def chunk(n):
        return pl.ds(pl.multiple_of(n * c, c), c)

    def bwd_one(n, s):
        sb_ref[n] = s.astype(BF16)
        kt = k_ref[chunk(n), :].astype(F32).T
        v = v_ref[chunk(n), :]
        kvf_ref[n] = _dot((kt * zeta_f).astype(BF16), v)
        return g_b * s + _dot((kt * zeta_b).astype(BF16), v)

    def bwd(t, s):
        for u in range(RET_UNROLL):
            s = bwd_one(N_CHUNKS - 1 - (t * RET_UNROLL + u), s)
        return s

    lax.fori_loop(0, N_CHUNKS // RET_UNROLL, bwd, s0_b)

    n_groups = N_CHUNKS // RET_GROUP

    def fwd_scores(t, a_ref):
        for u in range(RET_GROUP):
            n = t * RET_GROUP + u
            a_ref[u] = _dot_nt(q_ref[chunk(n), :], k_ref[chunk(n), :])

    def fwd_out(t, a_ref, s):
        for u in range(RET_GROUP):
            n = t * RET_GROUP + u
            q = q_ref[chunk(n), :]
            scores = (a_ref[u] * decay).astype(BF16)
            cross = _dot(q, jnp.concatenate([s.astype(BF16), sb_ref[n]], axis=1))
            o = (_dot(scores, v_ref[chunk(n), :])
                 + cross[:, :RET_V_DIM] * xi_f
                 + cross[:, RET_V_DIM:] * xi_b)
            mu = jnp.mean(o, axis=-1, keepdims=True)
            oc = o - mu
            var = jnp.mean(oc * oc, axis=-1, keepdims=True)
            gn = oc * lax.rsqrt(var + GN_EPS)
            o_ref[chunk(n), :] = (g_ref[chunk(n), :].astype(F32) * gn).astype(BF16)
            s = g_f * s + kvf_ref[n]
        return s

    fwd_scores(0, a0_ref)

    def two_groups(i, s):
        fwd_scores(2 * i + 1, a1_ref)
        s = fwd_out(2 * i, a0_ref, s)
        fwd_scores(2 * i + 2, a0_ref)
        return fwd_out(2 * i + 1, a1_ref, s)

    s = lax.fori_loop(0, n_groups // 2 - 1, two_groups, s0_f)
    fwd_scores(n_groups - 1, a1_ref)
    s = fwd_out(n_groups - 2, a0_ref, s)
    fwd_out(n_groups - 1, a1_ref, s)


def _ret(log_gamma, rproj, proj, gr, cproj):
    qk = lambda off: off // RET_QK_DIM
    vv = lambda off: off // RET_V_DIM
    return pl.pallas_call(
        _ret_kernel,
        out_shape=jax.ShapeDtypeStruct((N_LAT, RET_V_W), BF16),
        grid=(BATCH, RET_HEADS),
        in_specs=[
            pl.BlockSpec(memory_space=pltpu.SMEM),
            pl.BlockSpec((SEQ, RET_QK_DIM), lambda b, h: (b, qk(ROPE_QR) + h)),
            pl.BlockSpec((SEQ, RET_QK_DIM), lambda b, h: (b, qk(ROPE_KR) + h)),
            pl.BlockSpec((SEQ, RET_V_DIM), lambda b, h: (b, vv(PLAIN_VR) + h)),
            pl.BlockSpec((SEQ, RET_V_DIM), lambda b, h: (b, h)),
            pl.BlockSpec((CTX_LEN, RET_QK_DIM), lambda b, h: (b, qk(OFF_KR) + h)),
            pl.BlockSpec((CTX_LEN, RET_V_DIM), lambda b, h: (b, vv(OFF_VR) + h)),
        ],
        out_specs=pl.BlockSpec((SEQ, RET_V_DIM), lambda b, h: (b, h)),
        scratch_shapes=[pltpu.VMEM((N_CHUNKS, RET_QK_DIM, RET_V_DIM), BF16),
                        pltpu.VMEM((N_CHUNKS, RET_QK_DIM, RET_V_DIM), F32),
                        pltpu.VMEM((RET_GROUP, RET_CHUNK, RET_CHUNK), F32),
                        pltpu.VMEM((RET_GROUP, RET_CHUNK, RET_CHUNK), F32)],
        compiler_params=_params("parallel", "parallel"),
        name="retention",
    )(log_gamma, rproj, rproj, proj, gr, cproj, cproj)


def _merge_kernel(ona_ref, ret_ref, gna_ref, gret_ref, x_ref, mod_ref,
                  wna_ref, wret_ref, wout_ref, lng_ref, lnb_ref, *rest, n_cast):
    o_ref = rest[n_cast]
    _cast_blocks(rest[:n_cast], rest[n_cast + 1:])
    y_na = _dot(ona_ref[...], wna_ref[...])
    y_ret = _dot(ret_ref[...], wret_ref[...])
    z = (gna_ref[...].astype(F32) * y_na + gret_ref[...].astype(F32) * y_ret).astype(BF16)
    y = _dot(z, wout_ref[...])
    g = mod_ref[5:6, :]
    o_ref[...] = _layer_norm(ALPHA * x_ref[...] + g * y, lng_ref[...], lnb_ref[...], LN_EPS)


def _merge(o_na, ret, gates, x, mod, w_na, w_ret, w_out, ln_g, ln_b, *, cast=()):
    tm = MERGE_ROWS
    tiles_per_batch = SEQ // tm
    n_steps = N_LAT // tm
    resident = lambda shape: pl.BlockSpec(shape, lambda i: (0, 0), pipeline_mode=pl.Buffered(1))
    cast = tuple(cast)
    cast_specs = [_cast_specs(job, lambda i: i, n_steps) for job in cast]
    out = pl.pallas_call(
        functools.partial(_merge_kernel, n_cast=len(cast)),
        out_shape=[jax.ShapeDtypeStruct((N_LAT, D_MODEL), F32)] + [_cast_out_shape(job) for job in cast],
        grid=(n_steps,),
        in_specs=[
            pl.BlockSpec((tm, NA_W), lambda i: (i, 0)),
            pl.BlockSpec((tm, RET_V_W), lambda i: (i, 0)),
            pl.BlockSpec((tm, D_MODEL), lambda i: (i, 0)),
            pl.BlockSpec((tm, D_MODEL), lambda i: (i, 1)),
            pl.BlockSpec((tm, D_MODEL), lambda i: (i, 0)),
            pl.BlockSpec((None, N_MOD, D_MODEL), lambda i: (i // tiles_per_batch, 0, 0)),
            resident((NA_W, D_MODEL)),
            resident((RET_V_W, D_MODEL)),
            resident((D_MODEL, D_MODEL)),
            pl.BlockSpec((1, D_MODEL), lambda i: (0, 0)),
            pl.BlockSpec((1, D_MODEL), lambda i: (0, 0)),
        ] + [src for src, _ in cast_specs],
        out_specs=[pl.BlockSpec((tm, D_MODEL), lambda i: (i, 0))] + [dst for _, dst in cast_specs],
        compiler_params=_params("arbitrary"),
        name="merge",
    )(o_na, ret, gates, gates, x, mod, w_na, w_ret, w_out, ln_g, ln_b, *[job.w for job in cast])
    return out if cast else out[0]


def _rope_tables():
    nfreq = RET_QK_DIM // 4
    pos = jnp.arange(SEQ)
    prow = (pos // GRID_W).astype(F32)
    pcol = (pos % GRID_W).astype(F32)
    freqs = ROPE_BASE ** (-jnp.arange(nfreq, dtype=F32) / nfreq)
    ang = jnp.stack([prow, pcol], axis=-1)[:, :, None] * freqs
    cos = jnp.cos(ang)
    sin = jnp.sin(ang)
    cos_t = jnp.stack([cos, cos], axis=2).reshape(SEQ, RET_QK_DIM)
    sin_t = jnp.stack([-sin, sin], axis=2).reshape(SEQ, RET_QK_DIM)
    return cos_t, sin_t


def _na_bias_table(rpb):
    wc = NA_WIN_COLS
    n_dc = 2 * wc - 1
    col = jnp.arange(GRID_W)
    col_start = jnp.clip(col - wc // 2, 0, GRID_W - wc)
    col_mask = (col[None, :] >= col_start[:, None]) & (col[None, :] < col_start[:, None] + wc)
    dc_idx = jnp.clip(col[None, :] - col[:, None] + wc - 1, 0, n_dc - 1)
    onehot = ((dc_idx[None] == jnp.arange(n_dc)[:, None, None]) & col_mask[None]).astype(F32)
    t = jnp.einsum('hdc,cqk->hdqk', rpb, onehot, precision=lax.Precision.HIGHEST)
    t = t + jnp.where(col_mask, 0.0, NEG_INF)
    n_dr = 2 * NA_WIN_ROWS - 1
    t = t.reshape(NA_HEADS // 2, 2, n_dr, GRID_W, GRID_W).transpose(0, 2, 1, 3, 4)
    t = t.reshape(NA_HEADS // 2, n_dr, 2 * GRID_W, GRID_W)
    return jnp.concatenate([t[:, :-1], t[:, 1:]], axis=-1)


def _column_scale(n_cols, lo, hi, value):
    col = jnp.arange(n_cols)
    return jnp.where((col >= lo) & (col < hi), value, 1.0).astype(F32).reshape(1, n_cols)


def kernel(x, c, ctx, c_ctx, w_ada, b_ada, ln_g, ln_b, ffn1_w_in, ffn1_w_out, ffn2_w_in, ffn2_w_out,
           w_mix_in, na_rpb, ret_decay_fwd, ret_decay_bwd, w_na_out, w_ret_out, w_mix_out):
    assert x.shape == (BATCH, SEQ, D_MODEL) and ctx.shape == (BATCH, CTX_LEN, D_MODEL)
    assert w_ada.shape == (DEPTH, D_MODEL, N_MOD * D_MODEL) and w_mix_in.shape == (DEPTH, D_MODEL, MIX_COLS)
    l = 0
    ctx_mod = BATCH
    cc = jnp.zeros((MOD_ROWS, D_MODEL), F32).at[:BATCH].set(c).at[ctx_mod].set(c_ctx)
    mod = _adaln(cc, w_ada[l], b_ada[l].reshape(1, -1)).reshape(MOD_ROWS, N_MOD, D_MODEL)

    row = lambda v: v.reshape(1, D_MODEL)
    xf = x.reshape(N_LAT, D_MODEL)
    cf = ctx.reshape(N_CTX, D_MODEL)
    w_na, w_ret, w_out = w_na_out[l].astype(BF16), w_ret_out[l].astype(BF16), w_mix_out[l].astype(BF16)
    _, hc, w1_in, w1_out = _ffn_convert(cf, mod, ffn1_w_in[l], ffn1_w_out[l], row(ln_g[l, 0]), row(ln_b[l, 0]),
                                        k0=0, mod_base=ctx_mod)
    x1, h1, w_mix = _ffn(xf, mod, w1_in, w1_out, row(ln_g[l, 0]), row(ln_b[l, 0]), k0=0, tm=FFN_ROWS,
                         rows_per_mod=SEQ, mod_base=0, mix_out=True,
                         cast=[_CastJob(w_mix_in[l], CAST_ROWS_FFN)])

    cos_t, sin_t = _rope_tables()
    rproj = _rope_proj(h1, w_mix, cos_t, sin_t)
    latent = dict(tm=PROJ_ROWS, tn=PROJ_COLS)
    ones = jnp.ones((1, GATE_COLS), F32)
    proj = _proj(h1, w_mix, _column_scale(PLAIN_COLS, PLAIN_QA, PLAIN_COLS, NA_HEAD_DIM ** -0.5),
                 kind=KIND_SCALE, n_cols=PLAIN_COLS,
                 w_block=lambda j: j + (j >= OFF_KR // PROJ_COLS).astype(jnp.int32), **latent)
    gr = _proj(h1, w_mix, ones, kind=KIND_SILU, n_cols=RET_V_W,
               w_block=lambda j: OFF_GR // PROJ_COLS + j, **latent)
    gates = _proj(h1, w_mix, ones, kind=KIND_SIGMOID, n_cols=GATE_COLS,
                  w_block=lambda j: OFF_GNA // PROJ_COLS + j, **latent)
    cproj = _proj(hc, w_mix, _column_scale(KV_COLS, OFF_KR, OFF_VR, RET_QK_DIM ** -0.5),
                  kind=KIND_SCALE, n_cols=KV_COLS, tm=N_CTX, tn=CTX_PROJ_COLS, w_block=lambda j: j)

    o_na = _na(proj, cproj, _na_bias_table(na_rpb[l]))
    log_gamma = jnp.stack([jax.nn.log_sigmoid(ret_decay_fwd[l].astype(F32)),
                           jax.nn.log_sigmoid(ret_decay_bwd[l].astype(F32))])
    ret = _ret(log_gamma, rproj, proj, gr, cproj)

    x2, w2_in, w2_out = _merge(o_na, ret, gates, x1, mod, w_na, w_ret, w_out, row(ln_g[l, 1]), row(ln_b[l, 1]),
                               cast=[_CastJob(ffn2_w_in[l], CAST_ROWS_MERGE_IN, swiglu=True),
                                     _CastJob(ffn2_w_out[l], CAST_ROWS_MERGE_OUT)])
    x3 = _ffn(x2, mod, w2_in, w2_out, row(ln_g[l, 2]), row(ln_b[l, 2]), k0=6, tm=FFN_ROWS,
              rows_per_mod=SEQ, mod_base=0)
    return x3.reshape(BATCH, SEQ, D_MODEL)
```

```python
import functools
from typing import NamedTuple

import jax
import jax.numpy as jnp
from jax import lax
from jax.experimental import pallas as pl
from jax.experimental.pallas import tpu as pltpu

D_MODEL = 2048
BATCH = 2
SEQ = 8192
GRID_W = 64
CTX_LEN = 256
NA_HEADS = 16
NA_HEAD_DIM = 64
NA_W = NA_HEADS * NA_HEAD_DIM
NA_WIN_ROWS = 8
NA_WIN_COLS = 16
RET_HEADS = 8
RET_QK_DIM = 128
RET_V_DIM = 256
RET_QK_W = RET_HEADS * RET_QK_DIM
RET_V_W = RET_HEADS * RET_V_DIM
RET_CHUNK = 128
D_FF = 5632
N_MOD = 9
ROPE_BASE = 10000.0
LN_EPS = 1e-5
GN_EPS = 1e-6
NEG_INF = -1e30
DEPTH = 1
ALPHA = (2 * DEPTH) ** 0.25
MIX_COLS = 2 * NA_W + RET_QK_W + RET_V_W + NA_W + RET_QK_W + RET_V_W + 2 * D_MODEL
KV_COLS = 2 * NA_W + RET_QK_W + RET_V_W

OFF_KA = 0
OFF_VA = NA_W
OFF_KR = 2 * NA_W
OFF_VR = OFF_KR + RET_QK_W
OFF_QA = OFF_VR + RET_V_W
OFF_QR = OFF_QA + NA_W
OFF_GR = OFF_QR + RET_QK_W
OFF_GNA = OFF_GR + RET_V_W
OFF_GRET = OFF_GNA + D_MODEL

ROPE_COLS = 2 * RET_QK_W
ROPE_KR = 0
ROPE_QR = RET_QK_W
PLAIN_KA = 0
PLAIN_VA = PLAIN_KA + NA_W
PLAIN_VR = PLAIN_VA + NA_W
PLAIN_QA = PLAIN_VR + RET_V_W
PLAIN_COLS = PLAIN_QA + NA_W
GATE_COLS = 2 * D_MODEL

KIND_SCALE = 0
KIND_SILU = 1
KIND_SIGMOID = 2

N_LAT = BATCH * SEQ
N_CTX = BATCH * CTX_LEN
GRID_ROWS = SEQ // GRID_W
N_CHUNKS = SEQ // RET_CHUNK

V7X_VMEM_BYTES = 64 * 1024 * 1024
VMEM_LIMIT_BYTES = V7X_VMEM_BYTES - 8 * 1024 * 1024
LANES = 128

FFN_ROWS = 512
FFN_COLS = 512
FFN_SUB = 256
FFN_FINISH_ROWS = 256
N_SUB = D_FF // FFN_SUB
PROJ_ROWS = 1024
PROJ_COLS = 1024
CTX_PROJ_COLS = 2560
MERGE_ROWS = 256
ADA_COLS = 1024
MOD_ROWS = 8
NA_GROUP = 2
RET_GROUP = 4
RET_UNROLL = 8
CAST_ROWS_FFN = 16
CAST_ROWS_MERGE_IN = 32
CAST_ROWS_MERGE_OUT = 176

BF16 = jnp.bfloat16
F32 = jnp.float32


def _dot(a, b):
    return jnp.dot(a, b, preferred_element_type=F32)


def _dot_nt(a, b):
    return lax.dot_general(a, b, (((1,), (1,)), ((), ())), preferred_element_type=F32)


def _dot_tn(a, b):
    return lax.dot_general(a, b, (((0,), (0,)), ((), ())), preferred_element_type=F32)


def _sigmoid(x):
    return 0.5 * jnp.tanh(0.5 * x) + 0.5


def _layer_norm(y, g, b, eps):
    mu = jnp.mean(y, axis=-1, keepdims=True)
    yc = y - mu
    var = jnp.mean(yc * yc, axis=-1, keepdims=True)
    return yc * lax.rsqrt(var + eps) * g + b


def _params(*semantics):
    return pltpu.CompilerParams(dimension_semantics=semantics, vmem_limit_bytes=VMEM_LIMIT_BYTES)


def _swiglu_layout_shape(w):
    return (N_SUB, w.shape[0], 2 * FFN_SUB)


class _CastJob(NamedTuple):
    w: jax.Array
    rows: int
    swiglu: bool = False


def _cast_out_shape(job):
    shape = _swiglu_layout_shape(job.w) if job.swiglu else job.w.shape
    return jax.ShapeDtypeStruct(shape, BF16)


def _cast_specs(job, step, n_steps):
    n_rows, n_cols = job.w.shape
    n_blocks = n_rows // job.rows
    assert n_rows % job.rows == 0 and job.rows % 16 == 0 and n_blocks <= n_steps
    block = lambda *g: jnp.minimum(step(*g), n_blocks - 1)
    src = pl.BlockSpec((job.rows, n_cols), lambda *g: (block(*g), 0))
    if job.swiglu:
        return src, pl.BlockSpec((N_SUB, job.rows, 2 * FFN_SUB), lambda *g: (0, block(*g), 0))
    return src, src


def _cast_blocks(src_refs, dst_refs):
    for src, dst in zip(src_refs, dst_refs):
        w = src[...].astype(BF16)
        if len(dst.shape) == 2:
            dst[...] = w
        else:
            for k in range(N_SUB):
                dst[k, :, :FFN_SUB] = w[:, k * FFN_SUB:(k + 1) * FFN_SUB]
                dst[k, :, FFN_SUB:] = w[:, D_FF + k * FFN_SUB:D_FF + (k + 1) * FFN_SUB]


def _adaln_kernel(c_ref, w_ref, b_ref, o_ref):
    cc = c_ref[...]
    s = (cc * _sigmoid(cc)).astype(BF16)
    o_ref[...] = _dot(s, w_ref[...].astype(BF16)) + b_ref[...]


def _adaln(cc, w_ada, b_ada):
    n_out = N_MOD * D_MODEL
    return pl.pallas_call(
        _adaln_kernel,
        out_shape=jax.ShapeDtypeStruct((MOD_ROWS, n_out), F32),
        grid=(n_out // ADA_COLS,),
        in_specs=[
            pl.BlockSpec((MOD_ROWS, D_MODEL), lambda j: (0, 0)),
            pl.BlockSpec((D_MODEL, ADA_COLS), lambda j: (0, j)),
            pl.BlockSpec((1, ADA_COLS), lambda j: (0, j)),
        ],
        out_specs=pl.BlockSpec((MOD_ROWS, ADA_COLS), lambda j: (0, j)),
        compiler_params=_params("parallel"),
        name="adaln",
    )(cc, w_ada, b_ada)


def _ffn_first_step(partial, x_ref, mod_ref, h_ref, o_ref, k0):
    shift = mod_ref[k0:k0 + 1, :]
    scale = mod_ref[k0 + 1:k0 + 2, :]
    for c in range(o_ref.shape[0] // FFN_FINISH_ROWS):
        rows = slice(c * FFN_FINISH_ROWS, (c + 1) * FFN_FINISH_ROWS)
        h = (x_ref[rows, :] * (1.0 + scale) + shift).astype(BF16)
        h_ref[rows, :] = h
        o_ref[rows, :] = partial(h)


def _ffn_last_step(partial, x_ref, mod_ref, lng_ref, lnb_ref, o_ref, mix_ref, h_ref, k0):
    g = mod_ref[k0 + 2:k0 + 3, :]
    for c in range(o_ref.shape[0] // FFN_FINISH_ROWS):
        rows = slice(c * FFN_FINISH_ROWS, (c + 1) * FFN_FINISH_ROWS)
        acc = o_ref[rows, :] + partial(h_ref[rows, :])
        y = ALPHA * x_ref[rows, :] + g * (0.5 * acc)
        out = _layer_norm(y, lng_ref[...], lnb_ref[...], LN_EPS)
        o_ref[rows, :] = out
        if mix_ref is not None:
            mix_ref[rows, :] = (out * (1.0 + mod_ref[4:5, :]) + mod_ref[3:4, :]).astype(BF16)


def _swiglu(gate_up):
    gate = gate_up[:, :FFN_SUB]
    up = gate_up[:, FFN_SUB:]
    return (gate * _sigmoid(gate) * up).astype(BF16)


def _ffn_kernel(x_ref, mod_ref, wgu_ref, wo_ref, lng_ref, lnb_ref, *rest, k0, n_cast, mix_out):
    cast_src = rest[:n_cast]
    outs = rest[n_cast:]
    o_ref = outs[0]
    mix_ref = outs[1] if mix_out else None
    cast_dst = outs[1 + mix_out:1 + mix_out + n_cast]
    h_ref = outs[1 + mix_out + n_cast]
    j = pl.program_id(1)

    last = pl.num_programs(1) - 1

    def partial(h):
        act = jnp.concatenate([_swiglu(_dot(h, wgu_ref[k])) for k in range(wgu_ref.shape[0])], axis=1)
        return _dot(act, wo_ref[...])

    @pl.when(j == 0)
    def _():
        _ffn_first_step(partial, x_ref, mod_ref, h_ref, o_ref, k0)
        _cast_blocks(cast_src, cast_dst)

    @pl.when((j > 0) & (j < last))
    def _():
        o_ref[...] += partial(h_ref[...])
        _cast_blocks(cast_src, cast_dst)

    @pl.when(j == last)
    def _():
        _ffn_last_step(partial, x_ref, mod_ref, lng_ref, lnb_ref, o_ref, mix_ref, h_ref, k0)
        _cast_blocks(cast_src, cast_dst)


def _ffn(x, mod, w_in, w_out, ln_g, ln_b, *, k0, tm, rows_per_mod, mod_base, mix_out=False, cast=()):
    n_rows = x.shape[0]
    row_block = pl.BlockSpec((tm, D_MODEL), lambda i, j: (i, 0))
    tf = FFN_COLS
    tiles_per_mod = rows_per_mod // tm
    n_col = D_FF // tf
    n_steps = (n_rows // tm) * n_col
    cast = tuple(cast)
    cast_specs = [_cast_specs(job, lambda i, j: i * n_col + j, n_steps) for job in cast]

    mix_shape = [jax.ShapeDtypeStruct((n_rows, D_MODEL), BF16)] if mix_out else []
    out = pl.pallas_call(
        functools.partial(_ffn_kernel, k0=k0, n_cast=len(cast), mix_out=mix_out),
        out_shape=[jax.ShapeDtypeStruct((n_rows, D_MODEL), F32)] + mix_shape
        + [_cast_out_shape(job) for job in cast],
        grid=(n_rows // tm, n_col),
        in_specs=[
            row_block,
            pl.BlockSpec((None, N_MOD, D_MODEL), lambda i, j: (mod_base + i // tiles_per_mod, 0, 0)),
            pl.BlockSpec((tf // FFN_SUB, D_MODEL, 2 * FFN_SUB), lambda i, j: (j, 0, 0)),
            pl.BlockSpec((tf, D_MODEL), lambda i, j: (j, 0)),
            pl.BlockSpec((1, D_MODEL), lambda i, j: (0, 0)),
            pl.BlockSpec((1, D_MODEL), lambda i, j: (0, 0)),
        ] + [src for src, _ in cast_specs],
        out_specs=[row_block] + mix_out * [row_block] + [dst for _, dst in cast_specs],
        scratch_shapes=[pltpu.VMEM((tm, D_MODEL), BF16)],
        compiler_params=_params("arbitrary", "arbitrary"),
        name="ffn",
    )(x, mod, w_in, w_out, ln_g, ln_b, *[job.w for job in cast])
    return out if len(out) > 1 else out[0]


def _ffn_convert_kernel(x_ref, mod_ref, wg_ref, wu_ref, wo_ref, lng_ref, lnb_ref,
                        o_ref, mix_ref, wgu_out_ref, wo_out_ref, h_ref, *, k0):
    j = pl.program_id(1)
    last = pl.num_programs(1) - 1

    def convert():
        wgu_out_ref[...] = jnp.concatenate([wg_ref[...].astype(BF16), wu_ref[...].astype(BF16)], axis=1)
        wo_out_ref[...] = wo_ref[...].astype(BF16)

    def partial(h):
        return _dot(_swiglu(_dot(h, wgu_out_ref[...])), wo_out_ref[...])

    @pl.when(j == 0)
    def _():
        convert()
        _ffn_first_step(partial, x_ref, mod_ref, h_ref, o_ref, k0)

    @pl.when((j > 0) & (j < last))
    def _():
        convert()
        o_ref[...] += partial(h_ref[...])

    @pl.when(j == last)
    def _():
        convert()
        _ffn_last_step(partial, x_ref, mod_ref, lng_ref, lnb_ref, o_ref, mix_ref, h_ref, k0)


def _ffn_convert(x, mod, w_in, w_out, ln_g, ln_b, *, k0, mod_base):
    n_rows = x.shape[0]
    return pl.pallas_call(
        functools.partial(_ffn_convert_kernel, k0=k0),
        out_shape=[jax.ShapeDtypeStruct((n_rows, D_MODEL), F32),
                   jax.ShapeDtypeStruct((n_rows, D_MODEL), BF16),
                   jax.ShapeDtypeStruct(_swiglu_layout_shape(w_in), BF16),
                   jax.ShapeDtypeStruct(w_out.shape, BF16)],
        grid=(1, N_SUB),
        in_specs=[
            pl.BlockSpec((n_rows, D_MODEL), lambda i, j: (0, 0)),
            pl.BlockSpec((None, N_MOD, D_MODEL), lambda i, j: (mod_base, 0, 0)),
            pl.BlockSpec((D_MODEL, FFN_SUB), lambda i, j: (0, j)),
            pl.BlockSpec((D_MODEL, FFN_SUB), lambda i, j: (0, j + N_SUB)),
            pl.BlockSpec((FFN_SUB, D_MODEL), lambda i, j: (j, 0)),
            pl.BlockSpec((1, D_MODEL), lambda i, j: (0, 0)),
            pl.BlockSpec((1, D_MODEL), lambda i, j: (0, 0)),
        ],
        out_specs=[pl.BlockSpec((n_rows, D_MODEL), lambda i, j: (0, 0)),
                   pl.BlockSpec((n_rows, D_MODEL), lambda i, j: (0, 0)),
                   pl.BlockSpec((None, D_MODEL, 2 * FFN_SUB), lambda i, j: (j, 0, 0)),
                   pl.BlockSpec((FFN_SUB, D_MODEL), lambda i, j: (j, 0))],
        scratch_shapes=[pltpu.VMEM((n_rows, D_MODEL), BF16)],
        compiler_params=_params("arbitrary", "arbitrary"),
        name="ffn_convert",
    )(x, mod, w_in, w_in, w_out, ln_g, ln_b)


def _swap_pairs(x, lane_is_first):
    return jnp.where(lane_is_first, pltpu.roll(x, LANES - 32, 1), pltpu.roll(x, 32, 1))


def _proj_kernel(h_ref, w_ref, scale_ref, o_ref, *, kind):
    acc = _dot(h_ref[...], w_ref[...])
    if kind == KIND_SCALE:
        out = acc * scale_ref[...]
    elif kind == KIND_SILU:
        out = acc * _sigmoid(acc)
    else:
        out = _sigmoid(acc)
    o_ref[...] = out.astype(BF16)


def _proj(h, w, scale, *, kind, n_cols, tm, tn, w_block):
    n_rows = h.shape[0]
    return pl.pallas_call(
        functools.partial(_proj_kernel, kind=kind),
        out_shape=jax.ShapeDtypeStruct((n_rows, n_cols), BF16),
        grid=(n_rows // tm, n_cols // tn),
        in_specs=[
            pl.BlockSpec((tm, D_MODEL), lambda i, j: (i, 0)),
            pl.BlockSpec((D_MODEL, tn), lambda i, j: (0, w_block(j))),
            pl.BlockSpec((1, tn), lambda i, j: (0, j)),
        ],
        out_specs=pl.BlockSpec((tm, tn), lambda i, j: (i, j)),
        compiler_params=_params("parallel", "parallel"),
        name="mix_proj",
    )(h, w, scale)


def _rope_proj_kernel(h_ref, wk_ref, wq_ref, cos_ref, sin_ref, o_ref):
    h = h_ref[...]
    cos = cos_ref[...]
    sin = sin_ref[...]
    lane = lax.broadcasted_iota(jnp.int32, (1, LANES), 1)
    first = (lane & 32) == 0
    for w_ref, base, scale in ((wk_ref, ROPE_KR, RET_QK_DIM ** -0.5), (wq_ref, ROPE_QR, None)):
        acc = _dot(h, w_ref[...])
        for hd in range(RET_HEADS):
            xh = acc[:, hd * LANES:(hd + 1) * LANES]
            r = xh * cos + _swap_pairs(xh, first) * sin
            if scale is not None:
                r = r * scale
            o_ref[:, base + hd * LANES:base + (hd + 1) * LANES] = r.astype(BF16)


def _rope_proj(h, w, cos, sin):
    tm = PROJ_ROWS
    tiles_per_batch = SEQ // tm
    w_cols = lambda off: pl.BlockSpec((D_MODEL, RET_QK_W), lambda i: (0, off // RET_QK_W),
                                      pipeline_mode=pl.Buffered(1))
    return pl.pallas_call(
        _rope_proj_kernel,
        out_shape=jax.ShapeDtypeStruct((N_LAT, ROPE_COLS), BF16),
        grid=(N_LAT // tm,),
        in_specs=[
            pl.BlockSpec((tm, D_MODEL), lambda i: (i, 0)),
            w_cols(OFF_KR),
            w_cols(OFF_QR),
            pl.BlockSpec((tm, LANES), lambda i: (i % tiles_per_batch, 0)),
            pl.BlockSpec((tm, LANES), lambda i: (i % tiles_per_batch, 0)),
        ],
        out_specs=pl.BlockSpec((tm, ROPE_COLS), lambda i: (i, 0)),
        compiler_params=_params("parallel"),
        name="rope_proj",
    )(h, w, w, cos, sin)


def _na_kernel(q_ref, k_ref, v_ref, kc_ref, vc_ref, bias_ref, o_ref, s0_ref, c0_ref, s1_ref, c1_ref):
    lane = lax.broadcasted_iota(jnp.int32, (1, LANES), 1)
    first_head = lane < NA_HEAD_DIM
    zero = jnp.zeros((), BF16)
    kc = kc_ref[...]
    vc = vc_ref[...]
    win = NA_WIN_ROWS * GRID_W
    rows2 = 2 * GRID_W
    n_groups = GRID_ROWS // NA_GROUP
    slots = ((s0_ref, c0_ref), (s1_ref, c1_ref))

    def window_start(r):
        return jnp.clip(r - NA_WIN_ROWS // 2, 0, GRID_ROWS - NA_WIN_ROWS)

    def tokens(r, n):
        return pl.ds(pl.multiple_of(r * GRID_W, GRID_W), n)

    def scores(t, slot):
        s_ref, c_ref = slots[slot]
        q2s = []
        for u in range(NA_GROUP):
            r = t * NA_GROUP + u
            q = q_ref[tokens(r, GRID_W), :]
            q2 = jnp.concatenate([jnp.where(first_head, q, zero), jnp.where(first_head, zero, q)], axis=0)
            s_ref[u] = _dot_nt(q2, k_ref[tokens(window_start(r), win), :])
            q2s.append(q2)
        c_ref[...] = _dot_nt(jnp.concatenate(q2s, axis=0), kc)

    def attend(t, slot):
        s_ref, c_ref = slots[slot]
        o_lat, p_ctx, den = [], [], []
        for u in range(NA_GROUP):
            r = t * NA_GROUP + u
            rs = window_start(r)
            delta = r - rs
            bias = jnp.concatenate(
                [bias_ref[2 * m - delta + NA_WIN_ROWS - 1] for m in range(NA_WIN_ROWS // 2)], axis=1)
            s_lat = s_ref[u] + bias
            s_ctx = c_ref[u * rows2:(u + 1) * rows2, :]
            mx = jnp.maximum(jnp.max(s_lat, axis=-1, keepdims=True), jnp.max(s_ctx, axis=-1, keepdims=True))
            e_lat = jnp.exp(s_lat - mx)
            e_ctx = jnp.exp(s_ctx - mx)
            den.append(jnp.sum(e_lat, axis=-1, keepdims=True) + jnp.sum(e_ctx, axis=-1, keepdims=True))
            o_lat.append(_dot(e_lat.astype(BF16), v_ref[tokens(rs, win), :]))
            p_ctx.append(e_ctx.astype(BF16))
        o_ctx = _dot(jnp.concatenate(p_ctx, axis=0), vc)
        for u in range(NA_GROUP):
            o2 = (o_lat[u] + o_ctx[u * rows2:(u + 1) * rows2]) / den[u]
            out = jnp.where(first_head, o2[:GRID_W], o2[GRID_W:])
            o_ref[tokens(t * NA_GROUP + u, GRID_W), :] = out.astype(BF16)

    scores(0, 0)

    def two_groups(i, carry):
        scores(2 * i + 1, 1)
        attend(2 * i, 0)
        scores(2 * i + 2, 0)
        attend(2 * i + 1, 1)
        return carry

    lax.fori_loop(0, n_groups // 2 - 1, two_groups, 0)
    scores(n_groups - 1, 1)
    attend(n_groups - 2, 0)
    attend(n_groups - 1, 1)


def _na(proj, cproj, bias):
    n_pairs = NA_HEADS // 2
    blk = lambda off: off // LANES
    return pl.pallas_call(
        _na_kernel,
        out_shape=jax.ShapeDtypeStruct((N_LAT, NA_W), BF16),
        grid=(BATCH, n_pairs),
        in_specs=[
            pl.BlockSpec((SEQ, LANES), lambda b, p: (b, blk(PLAIN_QA) + p)),
            pl.BlockSpec((SEQ, LANES), lambda b, p: (b, blk(PLAIN_KA) + p)),
            pl.BlockSpec((SEQ, LANES), lambda b, p: (b, blk(PLAIN_VA) + p)),
            pl.BlockSpec((CTX_LEN, LANES), lambda b, p: (b, blk(OFF_KA) + p)),
            pl.BlockSpec((CTX_LEN, LANES), lambda b, p: (b, blk(OFF_VA) + p)),
            pl.BlockSpec((None, 2 * NA_WIN_ROWS - 2, 2 * GRID_W, 2 * GRID_W), lambda b, p: (p, 0, 0, 0)),
        ],
        out_specs=pl.BlockSpec((SEQ, LANES), lambda b, p: (b, p)),
        scratch_shapes=2 * [pltpu.VMEM((NA_GROUP, 2 * GRID_W, NA_WIN_ROWS * GRID_W), F32),
                            pltpu.VMEM((NA_GROUP * 2 * GRID_W, CTX_LEN), F32)],
        compiler_params=_params("parallel", "parallel"),
        name="nbr_attn",
    )(proj, proj, proj, cproj, cproj, bias)


def _ret_kernel(lg_ref, q_ref, k_ref, v_ref, g_ref, kc_ref, vc_ref, o_ref, sb_ref, kvf_ref, a0_ref, a1_ref):
    h = pl.program_id(1)
    lgf = lg_ref[0, h]
    lgb = lg_ref[1, h]
    c = RET_CHUNK
    ri = lax.broadcasted_iota(jnp.int32, (c, c), 0).astype(F32)
    ci = lax.broadcasted_iota(jnp.int32, (c, c), 1).astype(F32)
    diff = ri - ci
    decay = (jnp.where(diff >= 0, jnp.exp(lgf * jnp.maximum(diff, 0.0)), 0.0)
             + jnp.where(diff < 0, jnp.exp(lgb * jnp.maximum(-diff, 0.0)), 0.0))
    zeta_f = jnp.exp(lgf * (c - 1 - ci))
    zeta_b = jnp.exp(lgb * ci)
    rv = lax.broadcasted_iota(jnp.int32, (c, RET_V_DIM), 0).astype(F32)
    xi_f = jnp.exp(lgf * (rv + 1.0))
    xi_b = jnp.exp(lgb * (c - rv))
    g_f = jnp.exp(jnp.full((c, RET_V_DIM), lgf * c, F32))
    g_b = jnp.exp(jnp.full((c, RET_V_DIM), lgb * c, F32))

    pc = lax.broadcasted_iota(jnp.int32, (CTX_LEN, RET_QK_DIM), 0).astype(F32)
    kc = kc_ref[...].astype(F32)
    vc = vc_ref[...]
    s0_f = _dot_tn((kc * jnp.exp(lgf * (CTX_LEN - 1 - pc))).astype(BF16), vc)
    s0_b = _dot_tn((kc * jnp.exp(lgb * pc)).astype(BF16), vc)

    def chunk(n):
        return pl.ds(pl.multiple_of(n * c, c), c)

    def bwd_one(n, s):
        sb_ref[n] = s.astype(BF16)
        kt = k_ref[chunk(n), :].astype(F32).T
        v = v_ref[chunk(n), :]
        kvf_ref[n] = _dot((kt * zeta_f).astype(BF16), v)
        return g_b * s + _dot((kt * zeta_b).astype(BF16), v)

    def bwd(t, s):
        for u in range(RET_UNROLL):
            s = bwd_one(N_CHUNKS - 1 - (t * RET_UNROLL + u), s)
        return s

    lax.fori_loop(0, N_CHUNKS // RET_UNROLL, bwd, s0_b)

    n_groups = N_CHUNKS // RET_GROUP

    def fwd_scores(t, a_ref):
        for u in range(RET_GROUP):
            n = t * RET_GROUP + u
            a_ref[u] = _dot_nt(q_ref[chunk(n), :], k_ref[chunk(n), :])

    def fwd_out(t, a_ref, s):
        for u in range(RET_GROUP):
            n = t * RET_GROUP + u
            q = q_ref[chunk(n), :]
            scores = (a_ref[u] * decay).astype(BF16)
            cross = _dot(q, jnp.concatenate([s.astype(BF16), sb_ref[n]], axis=1))
            o = (_dot(scores, v_ref[chunk(n), :])
                 + cross[:, :RET_V_DIM] * xi_f
                 + cross[:, RET_V_DIM:] * xi_b)
            mu = jnp.mean(o, axis=-1, keepdims=True)
            oc = o - mu
            var = jnp.mean(oc * oc, axis=-1, keepdims=True)
            gn = oc * lax.rsqrt(var + GN_EPS)
            o_ref[chunk(n), :] = (g_ref[chunk(n), :].astype(F32) * gn).astype(BF16)
            s = g_f * s + kvf_ref[n]
        return s

    fwd_scores(0, a0_ref)

    def two_groups(i, s):
        fwd_scores(2 * i + 1, a1_ref)
        s = fwd_out(2 * i, a0_ref, s)
        fwd_scores(2 * i + 2, a0_ref)
        return fwd_out(2 * i + 1, a1_ref, s)

    s = lax.fori_loop(0, n_groups // 2 - 1, two_groups, s0_f)
    fwd_scores(n_groups - 1, a1_ref)
    s = fwd_out(n_groups - 2, a0_ref, s)
    fwd_out(n_groups - 1, a1_ref, s)


def _ret(log_gamma, rproj, proj, gr, cproj):
    qk = lambda off: off // RET_QK_DIM
    vv = lambda off: off // RET_V_DIM
    return pl.pallas_call(
        _ret_kernel,
        out_shape=jax.ShapeDtypeStruct((N_LAT, RET_V_W), BF16),
        grid=(BATCH, RET_HEADS),
        in_specs=[
            pl.BlockSpec(memory_space=pltpu.SMEM),
            pl.BlockSpec((SEQ, RET_QK_DIM), lambda b, h: (b, qk(ROPE_QR) + h)),
            pl.BlockSpec((SEQ, RET_QK_DIM), lambda b, h: (b, qk(ROPE_KR) + h)),
            pl.BlockSpec((SEQ, RET_V_DIM), lambda b, h: (b, vv(PLAIN_VR) + h)),
            pl.BlockSpec((SEQ, RET_V_DIM), lambda b, h: (b, h)),
            pl.BlockSpec((CTX_LEN, RET_QK_DIM), lambda b, h: (b, qk(OFF_KR) + h)),
            pl.BlockSpec((CTX_LEN, RET_V_DIM), lambda b, h: (b, vv(OFF_VR) + h)),
        ],
        out_specs=pl.BlockSpec((SEQ, RET_V_DIM), lambda b, h: (b, h)),
        scratch_shapes=[pltpu.VMEM((N_CHUNKS, RET_QK_DIM, RET_V_DIM), BF16),
                        pltpu.VMEM((N_CHUNKS, RET_QK_DIM, RET_V_DIM), F32),
                        pltpu.VMEM((RET_GROUP, RET_CHUNK, RET_CHUNK), F32),
                        pltpu.VMEM((RET_GROUP, RET_CHUNK, RET_CHUNK), F32)],
        compiler_params=_params("parallel", "parallel"),
        name="retention",
    )(log_gamma, rproj, rproj, proj, gr, cproj, cproj)


def _merge_kernel(ona_ref, ret_ref, gna_ref, gret_ref, x_ref, mod_ref,
                  wna_ref, wret_ref, wout_ref, lng_ref, lnb_ref, *rest, n_cast):
    o_ref = rest[n_cast]
    _cast_blocks(rest[:n_cast], rest[n_cast + 1:])
    y_na = _dot(ona_ref[...], wna_ref[...])
    y_ret = _dot(ret_ref[...], wret_ref[...])
    z = (gna_ref[...].astype(F32) * y_na + gret_ref[...].astype(F32) * y_ret).astype(BF16)
    y = _dot(z, wout_ref[...])
    g = mod_ref[5:6, :]
    o_ref[...] = _layer_norm(ALPHA * x_ref[...] + g * y, lng_ref[...], lnb_ref[...], LN_EPS)


def _merge(o_na, ret, gates, x, mod, w_na, w_ret, w_out, ln_g, ln_b, *, cast=()):
    tm = MERGE_ROWS
    tiles_per_batch = SEQ // tm
    n_steps = N_LAT // tm
    resident = lambda shape: pl.BlockSpec(shape, lambda i: (0, 0), pipeline_mode=pl.Buffered(1))
    cast = tuple(cast)
    cast_specs = [_cast_specs(job, lambda i: i, n_steps) for job in cast]
    out = pl.pallas_call(
        functools.partial(_merge_kernel, n_cast=len(cast)),
        out_shape=[jax.ShapeDtypeStruct((N_LAT, D_MODEL), F32)] + [_cast_out_shape(job) for job in cast],
        grid=(n_steps,),
        in_specs=[
            pl.BlockSpec((tm, NA_W), lambda i: (i, 0)),
            pl.BlockSpec((tm, RET_V_W), lambda i: (i, 0)),
            pl.BlockSpec((tm, D_MODEL), lambda i: (i, 0)),
            pl.BlockSpec((tm, D_MODEL), lambda i: (i, 1)),
            pl.BlockSpec((tm, D_MODEL), lambda i: (i, 0)),
            pl.BlockSpec((None, N_MOD, D_MODEL), lambda i: (i // tiles_per_batch, 0, 0)),
            resident((NA_W, D_MODEL)),
            resident((RET_V_W, D_MODEL)),
            resident((D_MODEL, D_MODEL)),
            pl.BlockSpec((1, D_MODEL), lambda i: (0, 0)),
            pl.BlockSpec((1, D_MODEL), lambda i: (0, 0)),
        ] + [src for src, _ in cast_specs],
        out_specs=[pl.BlockSpec((tm, D_MODEL), lambda i: (i, 0))] + [dst for _, dst in cast_specs],
        compiler_params=_params("arbitrary"),
        name="merge",
    )(o_na, ret, gates, gates, x, mod, w_na, w_ret, w_out, ln_g, ln_b, *[job.w for job in cast])
    return out if cast else out[0]


def _rope_tables():
    nfreq = RET_QK_DIM // 4
    pos = jnp.arange(SEQ)
    prow = (pos // GRID_W).astype(F32)
    pcol = (pos % GRID_W).astype(F32)
    freqs = ROPE_BASE ** (-jnp.arange(nfreq, dtype=F32) / nfreq)
    ang = jnp.stack([prow, pcol], axis=-1)[:, :, None] * freqs
    cos = jnp.cos(ang)
    sin = jnp.sin(ang)
    cos_t = jnp.stack([cos, cos], axis=2).reshape(SEQ, RET_QK_DIM)
    sin_t = jnp.stack([-sin, sin], axis=2).reshape(SEQ, RET_QK_DIM)
    return cos_t, sin_t


def _na_bias_table(rpb):
    wc = NA_WIN_COLS
    n_dc = 2 * wc - 1
    col = jnp.arange(GRID_W)
    col_start = jnp.clip(col - wc // 2, 0, GRID_W - wc)
    col_mask = (col[None, :] >= col_start[:, None]) & (col[None, :] < col_start[:, None] + wc)
    dc_idx = jnp.clip(col[None, :] - col[:, None] + wc - 1, 0, n_dc - 1)
    onehot = ((dc_idx[None] == jnp.arange(n_dc)[:, None, None]) & col_mask[None]).astype(F32)
    t = jnp.einsum('hdc,cqk->hdqk', rpb, onehot, precision=lax.Precision.HIGHEST)
    t = t + jnp.where(col_mask, 0.0, NEG_INF)
    n_dr = 2 * NA_WIN_ROWS - 1
    t = t.reshape(NA_HEADS // 2, 2, n_dr, GRID_W, GRID_W).transpose(0, 2, 1, 3, 4)
    t = t.reshape(NA_HEADS // 2, n_dr, 2 * GRID_W, GRID_W)
    return jnp.concatenate([t[:, :-1], t[:, 1:]], axis=-1)


def _column_scale(n_cols, lo, hi, value):
    col = jnp.arange(n_cols)
    return jnp.where((col >= lo) & (col < hi), value, 1.0).astype(F32).reshape(1, n_cols)


def kernel(x, c, ctx, c_ctx, w_ada, b_ada, ln_g, ln_b, ffn1_w_in, ffn1_w_out, ffn2_w_in, ffn2_w_out,
           w_mix_in, na_rpb, ret_decay_fwd, ret_decay_bwd, w_na_out, w_ret_out, w_mix_out):
    assert x.shape == (BATCH, SEQ, D_MODEL) and ctx.shape == (BATCH, CTX_LEN, D_MODEL)
    assert w_ada.shape == (DEPTH, D_MODEL, N_MOD * D_MODEL) and w_mix_in.shape == (DEPTH, D_MODEL, MIX_COLS)
    l = 0
    ctx_mod = BATCH
    cc = jnp.zeros((MOD_ROWS, D_MODEL), F32).at[:BATCH].set(c).at[ctx_mod].set(c_ctx)
    mod = _adaln(cc, w_ada[l], b_ada[l].reshape(1, -1)).reshape(MOD_ROWS, N_MOD, D_MODEL)

    row = lambda v: v.reshape(1, D_MODEL)
    xf = x.reshape(N_LAT, D_MODEL)
    cf = ctx.reshape(N_CTX, D_MODEL)
    w_na, w_ret, w_out = w_na_out[l].astype(BF16), w_ret_out[l].astype(BF16), w_mix_out[l].astype(BF16)
    _, hc, w1_in, w1_out = _ffn_convert(cf, mod, ffn1_w_in[l], ffn1_w_out[l], row(ln_g[l, 0]), row(ln_b[l, 0]),
                                        k0=0, mod_base=ctx_mod)
    x1, h1, w_mix = _ffn(xf, mod, w1_in, w1_out, row(ln_g[l, 0]), row(ln_b[l, 0]), k0=0, tm=FFN_ROWS,
                         rows_per_mod=SEQ, mod_base=0, mix_out=True,
                         cast=[_CastJob(w_mix_in[l], CAST_ROWS_FFN)])

    cos_t, sin_t = _rope_tables()
    rproj = _rope_proj(h1, w_mix, cos_t, sin_t)
    latent = dict(tm=PROJ_ROWS, tn=PROJ_COLS)
    ones = jnp.ones((1, GATE_COLS), F32)
    proj = _proj(h1, w_mix, _column_scale(PLAIN_COLS, PLAIN_QA, PLAIN_COLS, NA_HEAD_DIM ** -0.5),
                 kind=KIND_SCALE, n_cols=PLAIN_COLS,
                 w_block=lambda j: j + (j >= OFF_KR // PROJ_COLS).astype(jnp.int32), **latent)
    gr = _proj(h1, w_mix, ones, kind=KIND_SILU, n_cols=RET_V_W,
               w_block=lambda j: OFF_GR // PROJ_COLS + j, **latent)
    gates = _proj(h1, w_mix, ones, kind=KIND_SIGMOID, n_cols=GATE_COLS,
                  w_block=lambda j: OFF_GNA // PROJ_COLS + j, **latent)
    cproj = _proj(hc, w_mix, _column_scale(KV_COLS, OFF_KR, OFF_VR, RET_QK_DIM ** -0.5),
                  kind=KIND_SCALE, n_cols=KV_COLS, tm=N_CTX, tn=CTX_PROJ_COLS, w_block=lambda j: j)

    o_na = _na(proj, cproj, _na_bias_table(na_rpb[l]))
    log_gamma = jnp.stack([jax.nn.log_sigmoid(ret_decay_fwd[l].astype(F32)),
                           jax.nn.log_sigmoid(ret_decay_bwd[l].astype(F32))])
    ret = _ret(log_gamma, rproj, proj, gr, cproj)

    x2, w2_in, w2_out = _merge(o_na, ret, gates, x1, mod, w_na, w_ret, w_out, row(ln_g[l, 1]), row(ln_b[l, 1]),
                               cast=[_CastJob(ffn2_w_in[l], CAST_ROWS_MERGE_IN, swiglu=True),
                                     _CastJob(ffn2_w_out[l], CAST_ROWS_MERGE_OUT)])
    x3 = _ffn(x2, mod, w2_in, w2_out, row(ln_g[l, 2]), row(ln_b[l, 2]), k0=6, tm=FFN_ROWS,
              rows_per_mod=SEQ, mod_base=0)
    return x3.reshape(BATCH, SEQ, D_MODEL)
```

```python
import functools
from typing import NamedTuple

import jax
import jax.numpy as jnp
from jax import lax
from jax.experimental import pallas as pl
from jax.experimental.pallas import tpu as pltpu

D_MODEL = 2048
BATCH = 2
SEQ = 8192
GRID_W = 64
CTX_LEN = 256
NA_HEADS = 16
NA_HEAD_DIM = 64
NA_W = NA_HEADS * NA_HEAD_DIM
NA_WIN_ROWS = 8
NA_WIN_COLS = 16
RET_HEADS = 8
RET_QK_DIM = 128
RET_V_DIM = 256
RET_QK_W = RET_HEADS * RET_QK_DIM
RET_V_W = RET_HEADS * RET_V_DIM
RET_CHUNK = 128
D_FF = 5632
N_MOD = 9
ROPE_BASE = 10000.0
LN_EPS = 1e-5
GN_EPS = 1e-6
NEG_INF = -1e30
DEPTH = 1
ALPHA = (2 * DEPTH) ** 0.25
MIX_COLS = 2 * NA_W + RET_QK_W + RET_V_W + NA_W + RET_QK_W + RET_V_W + 2 * D_MODEL
KV_COLS = 2 * NA_W + RET_QK_W + RET_V_W

OFF_KA = 0
OFF_VA = NA_W
OFF_KR = 2 * NA_W
OFF_VR = OFF_KR + RET_QK_W
OFF_QA = OFF_VR + RET_V_W
OFF_QR = OFF_QA + NA_W
OFF_GR = OFF_QR + RET_QK_W
OFF_GNA = OFF_GR + RET_V_W
OFF_GRET = OFF_GNA + D_MODEL

ROPE_COLS = 2 * RET_QK_W
ROPE_KR = 0
ROPE_QR = RET_QK_W
PLAIN_KA = 0
PLAIN_VA = PLAIN_KA + NA_W
PLAIN_VR = PLAIN_VA + NA_W
PLAIN_QA = PLAIN_VR + RET_V_W
PLAIN_COLS = PLAIN_QA + NA_W
GATE_COLS = 2 * D_MODEL

KIND_SCALE = 0
KIND_SILU = 1
KIND_SIGMOID = 2

N_LAT = BATCH * SEQ
N_CTX = BATCH * CTX_LEN
GRID_ROWS = SEQ // GRID_W
N_CHUNKS = SEQ // RET_CHUNK

V7X_VMEM_BYTES = 64 * 1024 * 1024
VMEM_LIMIT_BYTES = V7X_VMEM_BYTES - 8 * 1024 * 1024
LANES = 128

FFN_ROWS = 512
FFN_COLS = 512
FFN_SUB = 256
FFN_FINISH_ROWS = 256
N_SUB = D_FF // FFN_SUB
PROJ_ROWS = 1024
PROJ_COLS = 1024
CTX_PROJ_COLS = 2560
MERGE_ROWS = 256
ADA_COLS = 1024
MOD_ROWS = 8
NA_GROUP = 2
RET_GROUP = 4
RET_UNROLL = 8
CAST_ROWS_FFN = 16
CAST_ROWS_MERGE_IN = 32
CAST_ROWS_MERGE_OUT = 176

BF16 = jnp.bfloat16
F32 = jnp.float32


def _dot(a, b):
    return jnp.dot(a, b, preferred_element_type=F32)


def _dot_nt(a, b):
    return lax.dot_general(a, b, (((1,), (1,)), ((), ())), preferred_element_type=F32)


def _dot_tn(a, b):
    return lax.dot_general(a, b, (((0,), (0,)), ((), ())), preferred_element_type=F32)


def _sigmoid(x):
    return 0.5 * jnp.tanh(0.5 * x) + 0.5


def _layer_norm(y, g, b, eps):
    mu = jnp.mean(y, axis=-1, keepdims=True)
    yc = y - mu
    var = jnp.mean(yc * yc, axis=-1, keepdims=True)
    return yc * lax.rsqrt(var + eps) * g + b


def _params(*semantics):
    return pltpu.CompilerParams(dimension_semantics=semantics, vmem_limit_bytes=VMEM_LIMIT_BYTES)


def _swiglu_layout_shape(w):
    return (N_SUB, w.shape[0], 2 * FFN_SUB)


class _CastJob(NamedTuple):
    w: jax.Array
    rows: int
    swiglu: bool = False


def _cast_out_shape(job):
    shape = _swiglu_layout_shape(job.w) if job.swiglu else job.w.shape
    return jax.ShapeDtypeStruct(shape, BF16)


def _cast_specs(job, step, n_steps):
    n_rows, n_cols = job.w.shape
    n_blocks = n_rows // job.rows
    assert n_rows % job.rows == 0 and job.rows % 16 == 0 and n_blocks <= n_steps
    block = lambda *g: jnp.minimum(step(*g), n_blocks - 1)
    src = pl.BlockSpec((job.rows, n_cols), lambda *g: (block(*g), 0))
    if job.swiglu:
        return src, pl.BlockSpec((N_SUB, job.rows, 2 * FFN_SUB), lambda *g: (0, block(*g), 0))
    return src, src


def _cast_blocks(src_refs, dst_refs):
    for src, dst in zip(src_refs, dst_refs):
        w = src[...].astype(BF16)
        if len(dst.shape) == 2:
            dst[...] = w
        else:
            for k in range(N_SUB):
                dst[k, :, :FFN_SUB] = w[:, k * FFN_SUB:(k + 1) * FFN_SUB]
                dst[k, :, FFN_SUB:] = w[:, D_FF + k * FFN_SUB:D_FF + (k + 1) * FFN_SUB]


def _adaln_kernel(c_ref, w_ref, b_ref, o_ref):
    cc = c_ref[...]
    s = (cc * _sigmoid(cc)).astype(BF16)
    o_ref[...] = _dot(s, w_ref[...].astype(BF16)) + b_ref[...]


def _adaln(cc, w_ada, b_ada):
    n_out = N_MOD * D_MODEL
    return pl.pallas_call(
        _adaln_kernel,
        out_shape=jax.ShapeDtypeStruct((MOD_ROWS, n_out), F32),
        grid=(n_out // ADA_COLS,),
        in_specs=[
            pl.BlockSpec((MOD_ROWS, D_MODEL), lambda j: (0, 0)),
            pl.BlockSpec((D_MODEL, ADA_COLS), lambda j: (0, j)),
            pl.BlockSpec((1, ADA_COLS), lambda j: (0, j)),
        ],
        out_specs=pl.BlockSpec((MOD_ROWS, ADA_COLS), lambda j: (0, j)),
        compiler_params=_params("parallel"),
        name="adaln",
    )(cc, w_ada, b_ada)


def _ffn_first_step(partial, x_ref, mod_ref, h_ref, o_ref, k0):
    shift = mod_ref[k0:k0 + 1, :]
    scale = mod_ref[k0 + 1:k0 + 2, :]
    for c in range(o_ref.shape[0] // FFN_FINISH_ROWS):
        rows = slice(c * FFN_FINISH_ROWS, (c + 1) * FFN_FINISH_ROWS)
        h = (x_ref[rows, :] * (1.0 + scale) + shift).astype(BF16)
        h_ref[rows, :] = h
        o_ref[rows, :] = partial(h)


def _ffn_last_step(partial, x_ref, mod_ref, lng_ref, lnb_ref, o_ref, mix_ref, h_ref, k0):
    g = mod_ref[k0 + 2:k0 + 3, :]
    for c in range(o_ref.shape[0] // FFN_FINISH_ROWS):
        rows = slice(c * FFN_FINISH_ROWS, (c + 1) * FFN_FINISH_ROWS)
        acc = o_ref[rows, :] + partial(h_ref[rows, :])
        y = ALPHA * x_ref[rows, :] + g * (0.5 * acc)
        out = _layer_norm(y, lng_ref[...], lnb_ref[...], LN_EPS)
        o_ref[rows, :] = out
        if mix_ref is not None:
            mix_ref[rows, :] = (out * (1.0 + mod_ref[4:5, :]) + mod_ref[3:4, :]).astype(BF16)


def _swiglu(gate_up):
    gate = gate_up[:, :FFN_SUB]
    up = gate_up[:, FFN_SUB:]
    return (gate * _sigmoid(gate) * up).astype(BF16)


def _ffn_kernel(x_ref, mod_ref, wgu_ref, wo_ref, lng_ref, lnb_ref, *rest, k0, n_cast, mix_out):
    cast_src = rest[:n_cast]
    outs = rest[n_cast:]
    o_ref = outs[0]
    mix_ref = outs[1] if mix_out else None
    cast_dst = outs[1 + mix_out:1 + mix_out + n_cast]
    h_ref = outs[1 + mix_out + n_cast]
    j = pl.program_id(1)

    last = pl.num_programs(1) - 1

    def partial(h):
        act = jnp.concatenate([_swiglu(_dot(h, wgu_ref[k])) for k in range(wgu_ref.shape[0])], axis=1)
        return _dot(act, wo_ref[...])

    @pl.when(j == 0)
    def _():
        _ffn_first_step(partial, x_ref, mod_ref, h_ref, o_ref, k0)
        _cast_blocks(cast_src, cast_dst)

    @pl.when((j > 0) & (j < last))
    def _():
        o_ref[...] += partial(h_ref[...])
        _cast_blocks(cast_src, cast_dst)

    @pl.when(j == last)
    def _():
        _ffn_last_step(partial, x_ref, mod_ref, lng_ref, lnb_ref, o_ref, mix_ref, h_ref, k0)
        _cast_blocks(cast_src, cast_dst)


def _ffn(x, mod, w_in, w_out, ln_g, ln_b, *, k0, tm, rows_per_mod, mod_base, mix_out=False, cast=()):
    n_rows = x.shape[0]
    row_block = pl.BlockSpec((tm, D_MODEL), lambda i, j: (i, 0))
    tf = FFN_COLS
    tiles_per_mod = rows_per_mod // tm
    n_col = D_FF // tf
    n_steps = (n_rows // tm) * n_col
    cast = tuple(cast)
    cast_specs = [_cast_specs(job, lambda i, j: i * n_col + j, n_steps) for job in cast]

    mix_shape = [jax.ShapeDtypeStruct((n_rows, D_MODEL), BF16)] if mix_out else []
    out = pl.pallas_call(
        functools.partial(_ffn_kernel, k0=k0, n_cast=len(cast), mix_out=mix_out),
        out_shape=[jax.ShapeDtypeStruct((n_rows, D_MODEL), F32)] + mix_shape
        + [_cast_out_shape(job) for job in cast],
        grid=(n_rows // tm, n_col),
        in_specs=[
            row_block,
            pl.BlockSpec((None, N_MOD, D_MODEL), lambda i, j: (mod_base + i // tiles_per_mod, 0, 0)),
            pl.BlockSpec((tf // FFN_SUB, D_MODEL, 2 * FFN_SUB), lambda i, j: (j, 0, 0)),
            pl.BlockSpec((tf, D_MODEL), lambda i, j: (j, 0)),
            pl.BlockSpec((1, D_MODEL), lambda i, j: (0, 0)),
            pl.BlockSpec((1, D_MODEL), lambda i, j: (0, 0)),
        ] + [src for src, _ in cast_specs],
        out_specs=[row_block] + mix_out * [row_block] + [dst for _, dst in cast_specs],
        scratch_shapes=[pltpu.VMEM((tm, D_MODEL), BF16)],
        compiler_params=_params("arbitrary", "arbitrary"),
        name="ffn",
    )(x, mod, w_in, w_out, ln_g, ln_b, *[job.w for job in cast])
    return out if len(out) > 1 else out[0]


def _ffn_convert_kernel(x_ref, mod_ref, wg_ref, wu_ref, wo_ref, lng_ref, lnb_ref,
                        o_ref, mix_ref, wgu_out_ref, wo_out_ref, h_ref, *, k0):
    j = pl.program_id(1)
    last = pl.num_programs(1) - 1

    def convert():
        wgu_out_ref[...] = jnp.concatenate([wg_ref[...].astype(BF16), wu_ref[...].astype(BF16)], axis=1)
        wo_out_ref[...] = wo_ref[...].astype(BF16)

    def partial(h):
        return _dot(_swiglu(_dot(h, wgu_out_ref[...])), wo_out_ref[...])

    @pl.when(j == 0)
    def _():
        convert()
        _ffn_first_step(partial, x_ref, mod_ref, h_ref, o_ref, k0)

    @pl.when((j > 0) & (j < last))
    def _():
        convert()
        o_ref[...] += partial(h_ref[...])

    @pl.when(j == last)
    def _():
        convert()
        _ffn_last_step(partial, x_ref, mod_ref, lng_ref, lnb_ref, o_ref, mix_ref, h_ref, k0)


def _ffn_convert(x, mod, w_in, w_out, ln_g, ln_b, *, k0, mod_base):
    n_rows = x.shape[0]
    return pl.pallas_call(
        functools.partial(_ffn_convert_kernel, k0=k0),
        out_shape=[jax.ShapeDtypeStruct((n_rows, D_MODEL), F32),
                   jax.ShapeDtypeStruct((n_rows, D_MODEL), BF16),
                   jax.ShapeDtypeStruct(_swiglu_layout_shape(w_in), BF16),
                   jax.ShapeDtypeStruct(w_out.shape, BF16)],
        grid=(1, N_SUB),
        in_specs=[
            pl.BlockSpec((n_rows, D_MODEL), lambda i, j: (0, 0)),
            pl.BlockSpec((None, N_MOD, D_MODEL), lambda i, j: (mod_base, 0, 0)),
            pl.BlockSpec((D_MODEL, FFN_SUB), lambda i, j: (0, j)),
            pl.BlockSpec((D_MODEL, FFN_SUB), lambda i, j: (0, j + N_SUB)),
            pl.BlockSpec((FFN_SUB, D_MODEL), lambda i, j: (j, 0)),
            pl.BlockSpec((1, D_MODEL), lambda i, j: (0, 0)),
            pl.BlockSpec((1, D_MODEL), lambda i, j: (0, 0)),
        ],
        out_specs=[pl.BlockSpec((n_rows, D_MODEL), lambda i, j: (0, 0)),
                   pl.BlockSpec((n_rows, D_MODEL), lambda i, j: (0, 0)),
                   pl.BlockSpec((None, D_MODEL, 2 * FFN_SUB), lambda i, j: (j, 0, 0)),
                   pl.BlockSpec((FFN_SUB, D_MODEL), lambda i, j: (j, 0))],
        scratch_shapes=[pltpu.VMEM((n_rows, D_MODEL), BF16)],
        compiler_params=_params("arbitrary", "arbitrary"),
        name="ffn_convert",
    )(x, mod, w_in, w_in, w_out, ln_g, ln_b)


def _swap_pairs(x, lane_is_first):
    return jnp.where(lane_is_first, pltpu.roll(x, LANES - 32, 1), pltpu.roll(x, 32, 1))


def _proj_kernel(h_ref, w_ref, scale_ref, o_ref, *, kind):
    acc = _dot(h_ref[...], w_ref[...])
    if kind == KIND_SCALE:
        out = acc * scale_ref[...]
    elif kind == KIND_SILU:
        out = acc * _sigmoid(acc)
    else:
        out = _sigmoid(acc)
    out = out.astype(BF16)
    for k in range(o_ref.shape[0]):
        o_ref[k] = out[:, k * LANES:(k + 1) * LANES]


def _proj(h, w, scale, *, kind, n_cols, tm, tn, w_block):
    n_rows = h.shape[0]
    return pl.pallas_call(
        functools.partial(_proj_kernel, kind=kind),
        out_shape=jax.ShapeDtypeStruct((n_cols // LANES, n_rows, LANES), BF16),
        grid=(n_rows // tm, n_cols // tn),
        in_specs=[
            pl.BlockSpec((tm, D_MODEL), lambda i, j: (i, 0)),
            pl.BlockSpec((D_MODEL, tn), lambda i, j: (0, w_block(j))),
            pl.BlockSpec((1, tn), lambda i, j: (0, j)),
        ],
        out_specs=pl.BlockSpec((tn // LANES, tm, LANES), lambda i, j: (j, i, 0)),
        compiler_params=_params("parallel", "parallel"),
        name="mix_proj",
    )(h, w, scale)


def _rope_proj_kernel(h_ref, wk_ref, wq_ref, cos_ref, sin_ref, o_ref):
    h = h_ref[...]
    cos = cos_ref[...]
    sin = sin_ref[...]
    lane = lax.broadcasted_iota(jnp.int32, (1, LANES), 1)
    first = (lane & 32) == 0
    for w_ref, base, scale in ((wk_ref, ROPE_KR, RET_QK_DIM ** -0.5), (wq_ref, ROPE_QR, None)):
        acc = _dot(h, w_ref[...])
        for hd in range(RET_HEADS):
            xh = acc[:, hd * LANES:(hd + 1) * LANES]
            r = xh * cos + _swap_pairs(xh, first) * sin
            if scale is not None:
                r = r * scale
            o_ref[base // LANES + hd] = r.astype(BF16)


def _rope_proj(h, w, cos, sin):
    tm = PROJ_ROWS
    tiles_per_batch = SEQ // tm
    w_cols = lambda off: pl.BlockSpec((D_MODEL, RET_QK_W), lambda i: (0, off // RET_QK_W),
                                      pipeline_mode=pl.Buffered(1))
    return pl.pallas_call(
        _rope_proj_kernel,
        out_shape=jax.ShapeDtypeStruct((ROPE_COLS // LANES, N_LAT, LANES), BF16),
        grid=(N_LAT // tm,),
        in_specs=[
            pl.BlockSpec((tm, D_MODEL), lambda i: (i, 0)),
            w_cols(OFF_KR),
            w_cols(OFF_QR),
            pl.BlockSpec((tm, LANES), lambda i: (i % tiles_per_batch, 0)),
            pl.BlockSpec((tm, LANES), lambda i: (i % tiles_per_batch, 0)),
        ],
        out_specs=pl.BlockSpec((ROPE_COLS // LANES, tm, LANES), lambda i: (0, i, 0)),
        compiler_params=_params("parallel"),
        name="rope_proj",
    )(h, w, w, cos, sin)


def _na_kernel(q_ref, k_ref, v_ref, kc_ref, vc_ref, bias_ref, o_ref, s0_ref, c0_ref, s1_ref, c1_ref):
    lane = lax.broadcasted_iota(jnp.int32, (1, LANES), 1)
    first_head = lane < NA_HEAD_DIM
    zero = jnp.zeros((), BF16)
    kc = kc_ref[...]
    vc = vc_ref[...]
    win = NA_WIN_ROWS * GRID_W
    rows2 = 2 * GRID_W
    n_groups = GRID_ROWS // NA_GROUP
    slots = ((s0_ref, c0_ref), (s1_ref, c1_ref))

    def window_start(r):
        return jnp.clip(r - NA_WIN_ROWS // 2, 0, GRID_ROWS - NA_WIN_ROWS)

    def tokens(r, n):
        return pl.ds(pl.multiple_of(r * GRID_W, GRID_W), n)

    def scores(t, slot):
        s_ref, c_ref = slots[slot]
        q2s = []
        for u in range(NA_GROUP):
            r = t * NA_GROUP + u
            q = q_ref[tokens(r, GRID_W), :]
            q2 = jnp.concatenate([jnp.where(first_head, q, zero), jnp.where(first_head, zero, q)], axis=0)
            s_ref[u] = _dot_nt(q2, k_ref[tokens(window_start(r), win), :])
            q2s.append(q2)
        c_ref[...] = _dot_nt(jnp.concatenate(q2s, axis=0), kc)

    def attend(t, slot):
        s_ref, c_ref = slots[slot]
        o_lat, p_ctx, den = [], [], []
        for u in range(NA_GROUP):
            r = t * NA_GROUP + u
            rs = window_start(r)
            delta = r - rs
            bias = jnp.concatenate(
                [bias_ref[2 * m - delta + NA_WIN_ROWS - 1] for m in range(NA_WIN_ROWS // 2)], axis=1)
            s_lat = s_ref[u] + bias
            s_ctx = c_ref[u * rows2:(u + 1) * rows2, :]
            mx = jnp.maximum(jnp.max(s_lat, axis=-1, keepdims=True), jnp.max(s_ctx, axis=-1, keepdims=True))
            e_lat = jnp.exp(s_lat - mx)
            e_ctx = jnp.exp(s_ctx - mx)
            den.append(jnp.sum(e_lat, axis=-1, keepdims=True) + jnp.sum(e_ctx, axis=-1, keepdims=True))
            o_lat.append(_dot(e_lat.astype(BF16), v_ref[tokens(rs, win), :]))
            p_ctx.append(e_ctx.astype(BF16))
        o_ctx = _dot(jnp.concatenate(p_ctx, axis=0), vc)
        for u in range(NA_GROUP):
            o2 = (o_lat[u] + o_ctx[u * rows2:(u + 1) * rows2]) / den[u]
            out = jnp.where(first_head, o2[:GRID_W], o2[GRID_W:])
            o_ref[tokens(t * NA_GROUP + u, GRID_W), :] = out.astype(BF16)

    scores(0, 0)

    def two_groups(i, carry):
        scores(2 * i + 1, 1)
        attend(2 * i, 0)
        scores(2 * i + 2, 0)
        attend(2 * i + 1, 1)
        return carry

    lax.fori_loop(0, n_groups // 2 - 1, two_groups, 0)
    scores(n_groups - 1, 1)
    attend(n_groups - 2, 0)
    attend(n_groups - 1, 1)


def _na(proj, cproj, bias):
    n_pairs = NA_HEADS // 2
    slab = lambda off: off // LANES
    return pl.pallas_call(
        _na_kernel,
        out_shape=jax.ShapeDtypeStruct((N_LAT, NA_W), BF16),
        grid=(BATCH, n_pairs),
        in_specs=[
            pl.BlockSpec((None, SEQ, LANES), lambda b, p: (slab(PLAIN_QA) + p, b, 0)),
            pl.BlockSpec((None, SEQ, LANES), lambda b, p: (slab(PLAIN_KA) + p, b, 0)),
            pl.BlockSpec((None, SEQ, LANES), lambda b, p: (slab(PLAIN_VA) + p, b, 0)),
            pl.BlockSpec((None, CTX_LEN, LANES), lambda b, p: (slab(OFF_KA) + p, b, 0)),
            pl.BlockSpec((None, CTX_LEN, LANES), lambda b, p: (slab(OFF_VA) + p, b, 0)),
            pl.BlockSpec((None, 2 * NA_WIN_ROWS - 2, 2 * GRID_W, 2 * GRID_W), lambda b, p: (p, 0, 0, 0)),
        ],
        out_specs=pl.BlockSpec((SEQ, LANES), lambda b, p: (b, p)),
        scratch_shapes=2 * [pltpu.VMEM((NA_GROUP, 2 * GRID_W, NA_WIN_ROWS * GRID_W), F32),
                            pltpu.VMEM((NA_GROUP * 2 * GRID_W, CTX_LEN), F32)],
        compiler_params=_params("parallel", "parallel"),
        name="nbr_attn",
    )(proj, proj, proj, cproj, cproj, bias)


def _ret_kernel(lg_ref, q_ref, k_ref, v_ref, g_ref, kc_ref, vc_ref, o_ref, sb_ref, kvf_ref, a0_ref, a1_ref):
    h = pl.program_id(1)
    lgf = lg_ref[0, h]
    lgb = lg_ref[1, h]
    c = RET_CHUNK
    ri = lax.broadcasted_iota(jnp.int32, (c, c), 0).astype(F32)
    ci = lax.broadcasted_iota(jnp.int32, (c, c), 1).astype(F32)
    diff = ri - ci
    decay = (jnp.where(diff >= 0, jnp.exp(lgf * jnp.maximum(diff, 0.0)), 0.0)
             + jnp.where(diff < 0, jnp.exp(lgb * jnp.maximum(-diff, 0.0)), 0.0))
    zeta_f = jnp.exp(lgf * (c - 1 - ci))
    zeta_b = jnp.exp(lgb * ci)
    rv = lax.broadcasted_iota(jnp.int32, (c, RET_V_DIM), 0).astype(F32)
    xi_f = jnp.exp(lgf * (rv + 1.0))
    xi_b = jnp.exp(lgb * (c - rv))
    g_f = jnp.exp(jnp.full((c, RET_V_DIM), lgf * c, F32))
    g_b = jnp.exp(jnp.full((c, RET_V_DIM), lgb * c, F32))

    pc = lax.broadcasted_iota(jnp.int32, (CTX_LEN, RET_QK_DIM), 0).astype(F32)
    def wide(ref, rows):
        return jnp.concatenate([ref[0, rows, :], ref[1, rows, :]], axis=1)

    kc = kc_ref[...].astype(F32)
    vc = wide(vc_ref, slice(None))
    s0_f = _dot_tn((kc * jnp.exp(lgf * (CTX_LEN - 1 - pc))).astype(BF16), vc)
    s0_b = _dot_tn((kc * jnp.exp(lgb * pc)).astype(BF16), vc)

    def chunk(n):
        return pl.ds(pl.multiple_of(n * c, c), c)

    def bwd_one(n, s):
        sb_ref[n] = s.astype(BF16)
        kt = k_ref[chunk(n), :].astype(F32).T
        v = wide(v_ref, chunk(n))
        kvf_ref[n] = _dot((kt * zeta_f).astype(BF16), v)
        return g_b * s + _dot((kt * zeta_b).astype(BF16), v)

    def bwd(t, s):
        for u in range(RET_UNROLL):
            s = bwd_one(N_CHUNKS - 1 - (t * RET_UNROLL + u), s)
        return s

    lax.fori_loop(0, N_CHUNKS // RET_UNROLL, bwd, s0_b)

    n_groups = N_CHUNKS // RET_GROUP

    def fwd_scores(t, a_ref):
        for u in range(RET_GROUP):
            n = t * RET_GROUP + u
            a_ref[u] = _dot_nt(q_ref[chunk(n), :], k_ref[chunk(n), :])

    def fwd_out(t, a_ref, s):
        for u in range(RET_GROUP):
            n = t * RET_GROUP + u
            q = q_ref[chunk(n), :]
            scores = (a_ref[u] * decay).astype(BF16)
            cross = _dot(q, jnp.concatenate([s.astype(BF16), sb_ref[n]], axis=1))
            o = (_dot(scores, wide(v_ref, chunk(n)))
                 + cross[:, :RET_V_DIM] * xi_f
                 + cross[:, RET_V_DIM:] * xi_b)
            mu = jnp.mean(o, axis=-1, keepdims=True)
            oc = o - mu
            var = jnp.mean(oc * oc, axis=-1, keepdims=True)
            gn = oc * lax.rsqrt(var + GN_EPS)
            o_ref[chunk(n), :] = (wide(g_ref, chunk(n)).astype(F32) * gn).astype(BF16)
            s = g_f * s + kvf_ref[n]
        return s

    fwd_scores(0, a0_ref)

    def two_groups(i, s):
        fwd_scores(2 * i + 1, a1_ref)
        s = fwd_out(2 * i, a0_ref, s)
        fwd_scores(2 * i + 2, a0_ref)
        return fwd_out(2 * i + 1, a1_ref, s)

    s = lax.fori_loop(0, n_groups // 2 - 1, two_groups, s0_f)
    fwd_scores(n_groups - 1, a1_ref)
    s = fwd_out(n_groups - 2, a0_ref, s)
    fwd_out(n_groups - 1, a1_ref, s)


def _ret(log_gamma, rproj, proj, gr, cproj):
    assert RET_QK_DIM == LANES and RET_V_DIM == 2 * LANES
    qk = lambda off: off // RET_QK_DIM
    vv = lambda off: off // RET_V_DIM
    return pl.pallas_call(
        _ret_kernel,
        out_shape=jax.ShapeDtypeStruct((N_LAT, RET_V_W), BF16),
        grid=(BATCH, RET_HEADS),
        in_specs=[
            pl.BlockSpec(memory_space=pltpu.SMEM),
            pl.BlockSpec((None, SEQ, LANES), lambda b, h: (qk(ROPE_QR) + h, b, 0)),
            pl.BlockSpec((None, SEQ, LANES), lambda b, h: (qk(ROPE_KR) + h, b, 0)),
            pl.BlockSpec((2, SEQ, LANES), lambda b, h: (vv(PLAIN_VR) + h, b, 0)),
            pl.BlockSpec((2, SEQ, LANES), lambda b, h: (h, b, 0)),
            pl.BlockSpec((None, CTX_LEN, LANES), lambda b, h: (qk(OFF_KR) + h, b, 0)),
            pl.BlockSpec((2, CTX_LEN, LANES), lambda b, h: (vv(OFF_VR) + h, b, 0)),
        ],
        out_specs=pl.BlockSpec((SEQ, RET_V_DIM), lambda b, h: (b, h)),
        scratch_shapes=[pltpu.VMEM((N_CHUNKS, RET_QK_DIM, RET_V_DIM), BF16),
                        pltpu.VMEM((N_CHUNKS, RET_QK_DIM, RET_V_DIM), F32),
                        pltpu.VMEM((RET_GROUP, RET_CHUNK, RET_CHUNK), F32),
                        pltpu.VMEM((RET_GROUP, RET_CHUNK, RET_CHUNK), F32)],
        compiler_params=_params("parallel", "parallel"),
        name="retention",
    )(log_gamma, rproj, rproj, proj, gr, cproj, cproj)


def _merge_kernel(ona_ref, ret_ref, gna_ref, gret_ref, x_ref, mod_ref,
                  wna_ref, wret_ref, wout_ref, lng_ref, lnb_ref, *rest, n_cast):
    o_ref = rest[n_cast]
    _cast_blocks(rest[:n_cast], rest[n_cast + 1:])
    y_na = _dot(ona_ref[...], wna_ref[...])
    y_ret = _dot(ret_ref[...], wret_ref[...])
    gate_na = jnp.concatenate([gna_ref[s] for s in range(gna_ref.shape[0])], axis=1).astype(F32)
    gate_ret = jnp.concatenate([gret_ref[s] for s in range(gret_ref.shape[0])], axis=1).astype(F32)
    z = (gate_na * y_na + gate_ret * y_ret).astype(BF16)
    y = _dot(z, wout_ref[...])
    g = mod_ref[5:6, :]
    o_ref[...] = _layer_norm(ALPHA * x_ref[...] + g * y, lng_ref[...], lnb_ref[...], LN_EPS)


def _merge(o_na, ret, gates, x, mod, w_na, w_ret, w_out, ln_g, ln_b, *, cast=()):
    tm = MERGE_ROWS
    tiles_per_batch = SEQ // tm
    n_steps = N_LAT // tm
    resident = lambda shape: pl.BlockSpec(shape, lambda i: (0, 0), pipeline_mode=pl.Buffered(1))
    cast = tuple(cast)
    cast_specs = [_cast_specs(job, lambda i: i, n_steps) for job in cast]
    out = pl.pallas_call(
        functools.partial(_merge_kernel, n_cast=len(cast)),
        out_shape=[jax.ShapeDtypeStruct((N_LAT, D_MODEL), F32)] + [_cast_out_shape(job) for job in cast],
        grid=(n_steps,),
        in_specs=[
            pl.BlockSpec((tm, NA_W), lambda i: (i, 0)),
            pl.BlockSpec((tm, RET_V_W), lambda i: (i, 0)),
            pl.BlockSpec((D_MODEL // LANES, tm, LANES), lambda i: (0, i, 0)),
            pl.BlockSpec((D_MODEL // LANES, tm, LANES), lambda i: (1, i, 0)),
            pl.BlockSpec((tm, D_MODEL), lambda i: (i, 0)),
            pl.BlockSpec((None, N_MOD, D_MODEL), lambda i: (i // tiles_per_batch, 0, 0)),
            resident((NA_W, D_MODEL)),
            resident((RET_V_W, D_MODEL)),
            resident((D_MODEL, D_MODEL)),
            pl.BlockSpec((1, D_MODEL), lambda i: (0, 0)),
            pl.BlockSpec((1, D_MODEL), lambda i: (0, 0)),
        ] + [src for src, _ in cast_specs],
        out_specs=[pl.BlockSpec((tm, D_MODEL), lambda i: (i, 0))] + [dst for _, dst in cast_specs],
        compiler_params=_params("arbitrary"),
        name="merge",
    )(o_na, ret, gates, gates, x, mod, w_na, w_ret, w_out, ln_g, ln_b, *[job.w for job in cast])
    return out if cast else out[0]


def _rope_tables():
    nfreq = RET_QK_DIM // 4
    pos = jnp.arange(SEQ)
    prow = (pos // GRID_W).astype(F32)
    pcol = (pos % GRID_W).astype(F32)
    freqs = ROPE_BASE ** (-jnp.arange(nfreq, dtype=F32) / nfreq)
    ang = jnp.stack([prow, pcol], axis=-1)[:, :, None] * freqs
    cos = jnp.cos(ang)
    sin = jnp.sin(ang)
    cos_t = jnp.stack([cos, cos], axis=2).reshape(SEQ, RET_QK_DIM)
    sin_t = jnp.stack([-sin, sin], axis=2).reshape(SEQ, RET_QK_DIM)
    return cos_t, sin_t


def _na_bias_table(rpb):
    wc = NA_WIN_COLS
    n_dc = 2 * wc - 1
    col = jnp.arange(GRID_W)
    col_start = jnp.clip(col - wc // 2, 0, GRID_W - wc)
    col_mask = (col[None, :] >= col_start[:, None]) & (col[None, :] < col_start[:, None] + wc)
    dc_idx = jnp.clip(col[None, :] - col[:, None] + wc - 1, 0, n_dc - 1)
    onehot = ((dc_idx[None] == jnp.arange(n_dc)[:, None, None]) & col_mask[None]).astype(F32)
    t = jnp.einsum('hdc,cqk->hdqk', rpb, onehot, precision=lax.Precision.HIGHEST)
    t = t + jnp.where(col_mask, 0.0, NEG_INF)
    n_dr = 2 * NA_WIN_ROWS - 1
    t = t.reshape(NA_HEADS // 2, 2, n_dr, GRID_W, GRID_W).transpose(0, 2, 1, 3, 4)
    t = t.reshape(NA_HEADS // 2, n_dr, 2 * GRID_W, GRID_W)
    return jnp.concatenate([t[:, :-1], t[:, 1:]], axis=-1)


def _column_scale(n_cols, lo, hi, value):
    col = jnp.arange(n_cols)
    return jnp.where((col >= lo) & (col < hi), value, 1.0).astype(F32).reshape(1, n_cols)


def kernel(x, c, ctx, c_ctx, w_ada, b_ada, ln_g, ln_b, ffn1_w_in, ffn1_w_out, ffn2_w_in, ffn2_w_out,
           w_mix_in, na_rpb, ret_decay_fwd, ret_decay_bwd, w_na_out, w_ret_out, w_mix_out):
    assert x.shape == (BATCH, SEQ, D_MODEL) and ctx.shape == (BATCH, CTX_LEN, D_MODEL)
    assert w_ada.shape == (DEPTH, D_MODEL, N_MOD * D_MODEL) and w_mix_in.shape == (DEPTH, D_MODEL, MIX_COLS)
    l = 0
    ctx_mod = BATCH
    cc = jnp.zeros((MOD_ROWS, D_MODEL), F32).at[:BATCH].set(c).at[ctx_mod].set(c_ctx)
    mod = _adaln(cc, w_ada[l], b_ada[l].reshape(1, -1)).reshape(MOD_ROWS, N_MOD, D_MODEL)

    row = lambda v: v.reshape(1, D_MODEL)
    xf = x.reshape(N_LAT, D_MODEL)
    cf = ctx.reshape(N_CTX, D_MODEL)
    w_na, w_ret, w_out = w_na_out[l].astype(BF16), w_ret_out[l].astype(BF16), w_mix_out[l].astype(BF16)
    _, hc, w1_in, w1_out = _ffn_convert(cf, mod, ffn1_w_in[l], ffn1_w_out[l], row(ln_g[l, 0]), row(ln_b[l, 0]),
                                        k0=0, mod_base=ctx_mod)
    x1, h1, w_mix = _ffn(xf, mod, w1_in, w1_out, row(ln_g[l, 0]), row(ln_b[l, 0]), k0=0, tm=FFN_ROWS,
                         rows_per_mod=SEQ, mod_base=0, mix_out=True,
                         cast=[_CastJob(w_mix_in[l], CAST_ROWS_FFN)])

    cos_t, sin_t = _rope_tables()
    rproj = _rope_proj(h1, w_mix, cos_t, sin_t)
    latent = dict(tm=PROJ_ROWS, tn=PROJ_COLS)
    ones = jnp.ones((1, GATE_COLS), F32)
    proj = _proj(h1, w_mix, _column_scale(PLAIN_COLS, PLAIN_QA, PLAIN_COLS, NA_HEAD_DIM ** -0.5),
                 kind=KIND_SCALE, n_cols=PLAIN_COLS,
                 w_block=lambda j: j + (j >= OFF_KR // PROJ_COLS).astype(jnp.int32), **latent)
    gr = _proj(h1, w_mix, ones, kind=KIND_SILU, n_cols=RET_V_W,
               w_block=lambda j: OFF_GR // PROJ_COLS + j, **latent)
    gates = _proj(h1, w_mix, ones, kind=KIND_SIGMOID, n_cols=GATE_COLS,
                  w_block=lambda j: OFF_GNA // PROJ_COLS + j, **latent)
    cproj = _proj(hc, w_mix, _column_scale(KV_COLS, OFF_KR, OFF_VR, RET_QK_DIM ** -0.5),
                  kind=KIND_SCALE, n_cols=KV_COLS, tm=N_CTX, tn=CTX_PROJ_COLS, w_block=lambda j: j)

    o_na = _na(proj, cproj, _na_bias_table(na_rpb[l]))
    log_gamma = jnp.stack([jax.nn.log_sigmoid(ret_decay_fwd[l].astype(F32)),
                           jax.nn.log_sigmoid(ret_decay_bwd[l].astype(F32))])
    ret = _ret(log_gamma, rproj, proj, gr, cproj)

    x2, w2_in, w2_out = _merge(o_na, ret, gates, x1, mod, w_na, w_ret, w_out, row(ln_g[l, 1]), row(ln_b[l, 1]),
                               cast=[_CastJob(ffn2_w_in[l], CAST_ROWS_MERGE_IN, swiglu=True),
                                     _CastJob(ffn2_w_out[l], CAST_ROWS_MERGE_OUT)])
    x3 = _ffn(x2, mod, w2_in, w2_out, row(ln_g[l, 2]), row(ln_b[l, 2]), k0=6, tm=FFN_ROWS,
              rows_per_mod=SEQ, mod_base=0)
    return x3.reshape(BATCH, SEQ, D_MODEL)
```

```python
import functools
from typing import NamedTuple

import jax
import jax.numpy as jnp
from jax import lax
from jax.experimental import pallas as pl
from jax.experimental.pallas import tpu as pltpu

D_MODEL = 2048
BATCH = 2
SEQ = 8192
GRID_W = 64
CTX_LEN = 256
NA_HEADS = 16
NA_HEAD_DIM = 64
NA_W = NA_HEADS * NA_HEAD_DIM
NA_WIN_ROWS = 8
NA_WIN_COLS = 16
RET_HEADS = 8
RET_QK_DIM = 128
RET_V_DIM = 256
RET_QK_W = RET_HEADS * RET_QK_DIM
RET_V_W = RET_HEADS * RET_V_DIM
RET_CHUNK = 128
D_FF = 5632
N_MOD = 9
ROPE_BASE = 10000.0
LN_EPS = 1e-5
GN_EPS = 1e-6
NEG_INF = -1e30
DEPTH = 1
ALPHA = (2 * DEPTH) ** 0.25
MIX_COLS = 2 * NA_W + RET_QK_W + RET_V_W + NA_W + RET_QK_W + RET_V_W + 2 * D_MODEL
KV_COLS = 2 * NA_W + RET_QK_W + RET_V_W

OFF_KA = 0
OFF_VA = NA_W
OFF_KR = 2 * NA_W
OFF_VR = OFF_KR + RET_QK_W
OFF_QA = OFF_VR + RET_V_W
OFF_QR = OFF_QA + NA_W
OFF_GR = OFF_QR + RET_QK_W
OFF_GNA = OFF_GR + RET_V_W
OFF_GRET = OFF_GNA + D_MODEL

ROPE_COLS = 2 * RET_QK_W
ROPE_KR = 0
ROPE_QR = RET_QK_W
PLAIN_KA = 0
PLAIN_VA = PLAIN_KA + NA_W
PLAIN_VR = PLAIN_VA + NA_W
PLAIN_QA = PLAIN_VR + RET_V_W
PLAIN_COLS = PLAIN_QA + NA_W
GATE_COLS = 2 * D_MODEL

KIND_SCALE = 0
KIND_SILU = 1
KIND_SIGMOID = 2

N_LAT = BATCH * SEQ
N_CTX = BATCH * CTX_LEN
GRID_ROWS = SEQ // GRID_W
N_CHUNKS = SEQ // RET_CHUNK

V7X_VMEM_BYTES = 64 * 1024 * 1024
VMEM_LIMIT_BYTES = V7X_VMEM_BYTES - 8 * 1024 * 1024
LANES = 128

FFN_ROWS = 512
FFN2_ROWS = 1024
FFN2_VMEM_LIMIT_BYTES = V7X_VMEM_BYTES - 3 * 1024 * 1024
FFN_COLS = 512
FFN_SUB = 256
FFN_FINISH_ROWS = 256
N_SUB = D_FF // FFN_SUB
PROJ_ROWS = 1024
PROJ_COLS = 1024
CTX_PROJ_COLS = 2560
MERGE_ROWS = 256
ADA_COLS = 1024
MOD_ROWS = 8
NA_GROUP = 2
RET_GROUP = 4
RET_UNROLL = 8
CAST_ROWS_FFN = 16
CAST_ROWS_MERGE_IN = 32
CAST_ROWS_MERGE_OUT = 176

BF16 = jnp.bfloat16
F32 = jnp.float32


def _dot(a, b):
    return jnp.dot(a, b, preferred_element_type=F32)


def _dot_nt(a, b):
    return lax.dot_general(a, b, (((1,), (1,)), ((), ())), preferred_element_type=F32)


def _dot_tn(a, b):
    return lax.dot_general(a, b, (((0,), (0,)), ((), ())), preferred_element_type=F32)


def _sigmoid(x):
    return 0.5 * jnp.tanh(0.5 * x) + 0.5


def _layer_norm(y, g, b, eps):
    mu = jnp.mean(y, axis=-1, keepdims=True)
    yc = y - mu
    var = jnp.mean(yc * yc, axis=-1, keepdims=True)
    return yc * lax.rsqrt(var + eps) * g + b


def _params(*semantics):
    return pltpu.CompilerParams(dimension_semantics=semantics, vmem_limit_bytes=VMEM_LIMIT_BYTES)


def _swiglu_layout_shape(w):
    return (N_SUB, w.shape[0], 2 * FFN_SUB)


class _CastJob(NamedTuple):
    w: jax.Array
    rows: int
    swiglu: bool = False


def _cast_out_shape(job):
    shape = _swiglu_layout_shape(job.w) if job.swiglu else job.w.shape
    return jax.ShapeDtypeStruct(shape, BF16)


def _cast_specs(job, step, n_steps):
    n_rows, n_cols = job.w.shape
    n_blocks = n_rows // job.rows
    assert n_rows % job.rows == 0 and job.rows % 16 == 0 and n_blocks <= n_steps
    block = lambda *g: jnp.minimum(step(*g), n_blocks - 1)
    src = pl.BlockSpec((job.rows, n_cols), lambda *g: (block(*g), 0))
    if job.swiglu:
        return src, pl.BlockSpec((N_SUB, job.rows, 2 * FFN_SUB), lambda *g: (0, block(*g), 0))
    return src, src


def _cast_blocks(src_refs, dst_refs):
    for src, dst in zip(src_refs, dst_refs):
        w = src[...].astype(BF16)
        if len(dst.shape) == 2:
            dst[...] = w
        else:
            for k in range(N_SUB):
                dst[k, :, :FFN_SUB] = w[:, k * FFN_SUB:(k + 1) * FFN_SUB]
                dst[k, :, FFN_SUB:] = w[:, D_FF + k * FFN_SUB:D_FF + (k + 1) * FFN_SUB]


def _adaln_kernel(c_ref, w_ref, b_ref, o_ref):
    cc = c_ref[...]
    s = (cc * _sigmoid(cc)).astype(BF16)
    o_ref[...] = _dot(s, w_ref[...].astype(BF16)) + b_ref[...]


def _adaln(cc, w_ada, b_ada):
    n_out = N_MOD * D_MODEL
    return pl.pallas_call(
        _adaln_kernel,
        out_shape=jax.ShapeDtypeStruct((MOD_ROWS, n_out), F32),
        grid=(n_out // ADA_COLS,),
        in_specs=[
            pl.BlockSpec((MOD_ROWS, D_MODEL), lambda j: (0, 0)),
            pl.BlockSpec((D_MODEL, ADA_COLS), lambda j: (0, j)),
            pl.BlockSpec((1, ADA_COLS), lambda j: (0, j)),
        ],
        out_specs=pl.BlockSpec((MOD_ROWS, ADA_COLS), lambda j: (0, j)),
        compiler_params=_params("parallel"),
        name="adaln",
    )(cc, w_ada, b_ada)


def _ffn_input(x_ref, mod_ref, k0, rows=slice(None)):
    shift = mod_ref[k0:k0 + 1, :]
    scale = mod_ref[k0 + 1:k0 + 2, :]
    return (x_ref[rows, :] * (1.0 + scale) + shift).astype(BF16)


def _ffn_first_step(partial, x_ref, mod_ref, h_ref, o_ref, k0):
    for c in range(o_ref.shape[0] // FFN_FINISH_ROWS):
        rows = slice(c * FFN_FINISH_ROWS, (c + 1) * FFN_FINISH_ROWS)
        h = _ffn_input(x_ref, mod_ref, k0, rows)
        if h_ref is not None:
            h_ref[rows, :] = h
        o_ref[rows, :] = partial(h)


def _ffn_last_step(partial, x_ref, mod_ref, lng_ref, lnb_ref, o_ref, mix_ref, h_ref, k0):
    g = mod_ref[k0 + 2:k0 + 3, :]
    for c in range(o_ref.shape[0] // FFN_FINISH_ROWS):
        rows = slice(c * FFN_FINISH_ROWS, (c + 1) * FFN_FINISH_ROWS)
        h = h_ref[rows, :] if h_ref is not None else _ffn_input(x_ref, mod_ref, k0, rows)
        acc = o_ref[rows, :] + partial(h)
        y = ALPHA * x_ref[rows, :] + g * (0.5 * acc)
        out = _layer_norm(y, lng_ref[...], lnb_ref[...], LN_EPS)
        o_ref[rows, :] = out
        if mix_ref is not None:
            mix_ref[rows, :] = (out * (1.0 + mod_ref[4:5, :]) + mod_ref[3:4, :]).astype(BF16)


def _swiglu(gate_up):
    gate = gate_up[:, :FFN_SUB]
    up = gate_up[:, FFN_SUB:]
    return (gate * _sigmoid(gate) * up).astype(BF16)


def _ffn_kernel(x_ref, mod_ref, wgu_ref, wo_ref, lng_ref, lnb_ref, *rest, k0, n_cast, mix_out, keep_h):
    cast_src = rest[:n_cast]
    outs = rest[n_cast:]
    o_ref = outs[0]
    mix_ref = outs[1] if mix_out else None
    cast_dst = outs[1 + mix_out:1 + mix_out + n_cast]
    h_ref = outs[1 + mix_out + n_cast] if keep_h else None
    j = pl.program_id(1)

    last = pl.num_programs(1) - 1

    def partial(h):
        act = jnp.concatenate([_swiglu(_dot(h, wgu_ref[k])) for k in range(wgu_ref.shape[0])], axis=1)
        return _dot(act, wo_ref[...])

    @pl.when(j == 0)
    def _():
        _ffn_first_step(partial, x_ref, mod_ref, h_ref, o_ref, k0)
        _cast_blocks(cast_src, cast_dst)

    @pl.when((j > 0) & (j < last))
    def _():
        o_ref[...] += partial(h_ref[...] if keep_h else _ffn_input(x_ref, mod_ref, k0))
        _cast_blocks(cast_src, cast_dst)

    @pl.when(j == last)
    def _():
        _ffn_last_step(partial, x_ref, mod_ref, lng_ref, lnb_ref, o_ref, mix_ref, h_ref, k0)
        _cast_blocks(cast_src, cast_dst)


def _ffn(x, mod, w_in, w_out, ln_g, ln_b, *, k0, tm, rows_per_mod, mod_base, mix_out=False, cast=(),
         keep_h=True, vmem_limit=VMEM_LIMIT_BYTES):
    n_rows = x.shape[0]
    row_block = pl.BlockSpec((tm, D_MODEL), lambda i, j: (i, 0))
    tf = FFN_COLS
    tiles_per_mod = rows_per_mod // tm
    n_col = D_FF // tf
    n_steps = (n_rows // tm) * n_col
    cast = tuple(cast)
    cast_specs = [_cast_specs(job, lambda i, j: i * n_col + j, n_steps) for job in cast]

    mix_shape = [jax.ShapeDtypeStruct((n_rows, D_MODEL), BF16)] if mix_out else []
    out = pl.pallas_call(
        functools.partial(_ffn_kernel, k0=k0, n_cast=len(cast), mix_out=mix_out, keep_h=keep_h),
        out_shape=[jax.ShapeDtypeStruct((n_rows, D_MODEL), F32)] + mix_shape
        + [_cast_out_shape(job) for job in cast],
        grid=(n_rows // tm, n_col),
        in_specs=[
            row_block,
            pl.BlockSpec((None, N_MOD, D_MODEL), lambda i, j: (mod_base + i // tiles_per_mod, 0, 0)),
            pl.BlockSpec((tf // FFN_SUB, D_MODEL, 2 * FFN_SUB), lambda i, j: (j, 0, 0)),
            pl.BlockSpec((tf, D_MODEL), lambda i, j: (j, 0)),
            pl.BlockSpec((1, D_MODEL), lambda i, j: (0, 0)),
            pl.BlockSpec((1, D_MODEL), lambda i, j: (0, 0)),
        ] + [src for src, _ in cast_specs],
        out_specs=[row_block] + mix_out * [row_block] + [dst for _, dst in cast_specs],
        scratch_shapes=keep_h * [pltpu.VMEM((tm, D_MODEL), BF16)],
        compiler_params=pltpu.CompilerParams(dimension_semantics=("arbitrary", "arbitrary"),
                                             vmem_limit_bytes=vmem_limit),
        name="ffn",
    )(x, mod, w_in, w_out, ln_g, ln_b, *[job.w for job in cast])
    return out if len(out) > 1 else out[0]


def _ffn_convert_kernel(x_ref, mod_ref, wg_ref, wu_ref, wo_ref, lng_ref, lnb_ref,
                        o_ref, mix_ref, wgu_out_ref, wo_out_ref, h_ref, *, k0):
    j = pl.program_id(1)
    last = pl.num_programs(1) - 1

    def convert():
        wgu_out_ref[...] = jnp.concatenate([wg_ref[...].astype(BF16), wu_ref[...].astype(BF16)], axis=1)
        wo_out_ref[...] = wo_ref[...].astype(BF16)

    def partial(h):
        return _dot(_swiglu(_dot(h, wgu_out_ref[...])), wo_out_ref[...])

    @pl.when(j == 0)
    def _():
        convert()
        _ffn_first_step(partial, x_ref, mod_ref, h_ref, o_ref, k0)

    @pl.when((j > 0) & (j < last))
    def _():
        convert()
        o_ref[...] += partial(h_ref[...])

    @pl.when(j == last)
    def _():
        convert()
        _ffn_last_step(partial, x_ref, mod_ref, lng_ref, lnb_ref, o_ref, mix_ref, h_ref, k0)


def _ffn_convert(x, mod, w_in, w_out, ln_g, ln_b, *, k0, mod_base):
    n_rows = x.shape[0]
    return pl.pallas_call(
        functools.partial(_ffn_convert_kernel, k0=k0),
        out_shape=[jax.ShapeDtypeStruct((n_rows, D_MODEL), F32),
                   jax.ShapeDtypeStruct((n_rows, D_MODEL), BF16),
                   jax.ShapeDtypeStruct(_swiglu_layout_shape(w_in), BF16),
                   jax.ShapeDtypeStruct(w_out.shape, BF16)],
        grid=(1, N_SUB),
        in_specs=[
            pl.BlockSpec((n_rows, D_MODEL), lambda i, j: (0, 0)),
            pl.BlockSpec((None, N_MOD, D_MODEL), lambda i, j: (mod_base, 0, 0)),
            pl.BlockSpec((D_MODEL, FFN_SUB), lambda i, j: (0, j)),
            pl.BlockSpec((D_MODEL, FFN_SUB), lambda i, j: (0, j + N_SUB)),
            pl.BlockSpec((FFN_SUB, D_MODEL), lambda i, j: (j, 0)),
            pl.BlockSpec((1, D_MODEL), lambda i, j: (0, 0)),
            pl.BlockSpec((1, D_MODEL), lambda i, j: (0, 0)),
        ],
        out_specs=[pl.BlockSpec((n_rows, D_MODEL), lambda i, j: (0, 0)),
                   pl.BlockSpec((n_rows, D_MODEL), lambda i, j: (0, 0)),
                   pl.BlockSpec((None, D_MODEL, 2 * FFN_SUB), lambda i, j: (j, 0, 0)),
                   pl.BlockSpec((FFN_SUB, D_MODEL), lambda i, j: (j, 0))],
        scratch_shapes=[pltpu.VMEM((n_rows, D_MODEL), BF16)],
        compiler_params=_params("arbitrary", "arbitrary"),
        name="ffn_convert",
    )(x, mod, w_in, w_in, w_out, ln_g, ln_b)


def _swap_pairs(x, lane_is_first):
    return jnp.where(lane_is_first, pltpu.roll(x, LANES - 32, 1), pltpu.roll(x, 32, 1))


def _proj_kernel(h_ref, w_ref, scale_ref, o_ref, *, kind):
    acc = _dot(h_ref[...], w_ref[...])
    if kind == KIND_SCALE:
        out = acc * scale_ref[...]
    elif kind == KIND_SILU:
        out = acc * _sigmoid(acc)
    else:
        out = _sigmoid(acc)
    o_ref[...] = out.astype(BF16)


def _proj(h, w, scale, *, kind, n_cols, tm, tn, w_block):
    n_rows = h.shape[0]
    return pl.pallas_call(
        functools.partial(_proj_kernel, kind=kind),
        out_shape=jax.ShapeDtypeStruct((n_rows, n_cols), BF16),
        grid=(n_rows // tm, n_cols // tn),
        in_specs=[
            pl.BlockSpec((tm, D_MODEL), lambda i, j: (i, 0)),
            pl.BlockSpec((D_MODEL, tn), lambda i, j: (0, w_block(j))),
            pl.BlockSpec((1, tn), lambda i, j: (0, j)),
        ],
        out_specs=pl.BlockSpec((tm, tn), lambda i, j: (i, j)),
        compiler_params=_params("parallel", "parallel"),
        name="mix_proj",
    )(h, w, scale)


def _rope_proj_kernel(h_ref, wk_ref, wq_ref, cos_ref, sin_ref, o_ref):
    h = h_ref[...]
    cos = cos_ref[...]
    sin = sin_ref[...]
    lane = lax.broadcasted_iota(jnp.int32, (1, LANES), 1)
    first = (lane & 32) == 0
    for w_ref, base, scale in ((wk_ref, ROPE_KR, RET_QK_DIM ** -0.5), (wq_ref, ROPE_QR, None)):
        acc = _dot(h, w_ref[...])
        for hd in range(RET_HEADS):
            xh = acc[:, hd * LANES:(hd + 1) * LANES]
            r = xh * cos + _swap_pairs(xh, first) * sin
            if scale is not None:
                r = r * scale
            o_ref[:, base + hd * LANES:base + (hd + 1) * LANES] = r.astype(BF16)


def _rope_proj(h, w, cos, sin):
    tm = PROJ_ROWS
    tiles_per_batch = SEQ // tm
    w_cols = lambda off: pl.BlockSpec((D_MODEL, RET_QK_W), lambda i: (0, off // RET_QK_W),
                                      pipeline_mode=pl.Buffered(1))
    return pl.pallas_call(
        _rope_proj_kernel,
        out_shape=jax.ShapeDtypeStruct((N_LAT, ROPE_COLS), BF16),
        grid=(N_LAT // tm,),
        in_specs=[
            pl.BlockSpec((tm, D_MODEL), lambda i: (i, 0)),
            w_cols(OFF_KR),
            w_cols(OFF_QR),
            pl.BlockSpec((tm, LANES), lambda i: (i % tiles_per_batch, 0)),
            pl.BlockSpec((tm, LANES), lambda i: (i % tiles_per_batch, 0)),
        ],
        out_specs=pl.BlockSpec((tm, ROPE_COLS), lambda i: (i, 0)),
        compiler_params=_params("parallel"),
        name="rope_proj",
    )(h, w, w, cos, sin)


def _na_kernel(q_ref, k_ref, v_ref, kc_ref, vc_ref, bias_ref, o_ref, s0_ref, c0_ref, s1_ref, c1_ref):
    lane = lax.broadcasted_iota(jnp.int32, (1, LANES), 1)
    first_head = lane < NA_HEAD_DIM
    zero = jnp.zeros((), BF16)
    kc = kc_ref[...]
    vc = vc_ref[...]
    win = NA_WIN_ROWS * GRID_W
    rows2 = 2 * GRID_W
    n_groups = GRID_ROWS // NA_GROUP
    slots = ((s0_ref, c0_ref), (s1_ref, c1_ref))

    def window_start(r):
        return jnp.clip(r - NA_WIN_ROWS // 2, 0, GRID_ROWS - NA_WIN_ROWS)

    def tokens(r, n):
        return pl.ds(pl.multiple_of(r * GRID_W, GRID_W), n)

    def scores(t, slot):
        s_ref, c_ref = slots[slot]
        q2s = []
        for u in range(NA_GROUP):
            r = t * NA_GROUP + u
            q = q_ref[tokens(r, GRID_W), :]
            q2 = jnp.concatenate([jnp.where(first_head, q, zero), jnp.where(first_head, zero, q)], axis=0)
            s_ref[u] = _dot_nt(q2, k_ref[tokens(window_start(r), win), :])
            q2s.append(q2)
        c_ref[...] = _dot_nt(jnp.concatenate(q2s, axis=0), kc)

    def attend(t, slot):
        s_ref, c_ref = slots[slot]
        o_lat, p_ctx, den = [], [], []
        for u in range(NA_GROUP):
            r = t * NA_GROUP + u
            rs = window_start(r)
            delta = r - rs
            bias = jnp.concatenate(
                [bias_ref[2 * m - delta + NA_WIN_ROWS - 1] for m in range(NA_WIN_ROWS // 2)], axis=1)
            s_lat = s_ref[u] + bias
            s_ctx = c_ref[u * rows2:(u + 1) * rows2, :]
            mx = jnp.maximum(jnp.max(s_lat, axis=-1, keepdims=True), jnp.max(s_ctx, axis=-1, keepdims=True))
            e_lat = jnp.exp(s_lat - mx)
            e_ctx = jnp.exp(s_ctx - mx)
            den.append(jnp.sum(e_lat, axis=-1, keepdims=True) + jnp.sum(e_ctx, axis=-1, keepdims=True))
            o_lat.append(_dot(e_lat.astype(BF16), v_ref[tokens(rs, win), :]))
            p_ctx.append(e_ctx.astype(BF16))
        o_ctx = _dot(jnp.concatenate(p_ctx, axis=0), vc)
        for u in range(NA_GROUP):
            o2 = (o_lat[u] + o_ctx[u * rows2:(u + 1) * rows2]) / den[u]
            out = jnp.where(first_head, o2[:GRID_W], o2[GRID_W:])
            o_ref[tokens(t * NA_GROUP + u, GRID_W), :] = out.astype(BF16)

    scores(0, 0)

    def two_groups(i, carry):
        scores(2 * i + 1, 1)
        attend(2 * i, 0)
        scores(2 * i + 2, 0)
        attend(2 * i + 1, 1)
        return carry

    lax.fori_loop(0, n_groups // 2 - 1, two_groups, 0)
    scores(n_groups - 1, 1)
    attend(n_groups - 2, 0)
    attend(n_groups - 1, 1)


def _na(proj, cproj, bias):
    n_pairs = NA_HEADS // 2
    blk = lambda off: off // LANES
    return pl.pallas_call(
        _na_kernel,
        out_shape=jax.ShapeDtypeStruct((N_LAT, NA_W), BF16),
        grid=(BATCH, n_pairs),
        in_specs=[
            pl.BlockSpec((SEQ, LANES), lambda b, p: (b, blk(PLAIN_QA) + p)),
            pl.BlockSpec((SEQ, LANES), lambda b, p: (b, blk(PLAIN_KA) + p)),
            pl.BlockSpec((SEQ, LANES), lambda b, p: (b, blk(PLAIN_VA) + p)),
            pl.BlockSpec((CTX_LEN, LANES), lambda b, p: (b, blk(OFF_KA) + p)),
            pl.BlockSpec((CTX_LEN, LANES), lambda b, p: (b, blk(OFF_VA) + p)),
            pl.BlockSpec((None, 2 * NA_WIN_ROWS - 2, 2 * GRID_W, 2 * GRID_W), lambda b, p: (p, 0, 0, 0)),
        ],
        out_specs=pl.BlockSpec((SEQ, LANES), lambda b, p: (b, p)),
        scratch_shapes=2 * [pltpu.VMEM((NA_GROUP, 2 * GRID_W, NA_WIN_ROWS * GRID_W), F32),
                            pltpu.VMEM((NA_GROUP * 2 * GRID_W, CTX_LEN), F32)],
        compiler_params=_params("parallel", "parallel"),
        name="nbr_attn",
    )(proj, proj, proj, cproj, cproj, bias)


def _ret_kernel(lg_ref, q_ref, k_ref, v_ref, g_ref, kc_ref, vc_ref, o_ref, sb_ref, kvf_ref, a0_ref, a1_ref):
    h = pl.program_id(1)
    lgf = lg_ref[0, h]
    lgb = lg_ref[1, h]
    c = RET_CHUNK
    ri = lax.broadcasted_iota(jnp.int32, (c, c), 0).astype(F32)
    ci = lax.broadcasted_iota(jnp.int32, (c, c), 1).astype(F32)
    diff = ri - ci
    decay = (jnp.where(diff >= 0, jnp.exp(lgf * jnp.maximum(diff, 0.0)), 0.0)
             + jnp.where(diff < 0, jnp.exp(lgb * jnp.maximum(-diff, 0.0)), 0.0))
    zeta_f = jnp.exp(lgf * (c - 1 - ci))
    zeta_b = jnp.exp(lgb * ci)
    rv = lax.broadcasted_iota(jnp.int32, (c, RET_V_DIM), 0).astype(F32)
    xi_f = jnp.exp(lgf * (rv + 1.0))
    xi_b = jnp.exp(lgb * (c - rv))
    g_f = jnp.exp(jnp.full((c, RET_V_DIM), lgf * c, F32))
    g_b = jnp.exp(jnp.full((c, RET_V_DIM), lgb * c, F32))

    pc = lax.broadcasted_iota(jnp.int32, (CTX_LEN, RET_QK_DIM), 0).astype(F32)
    kc = kc_ref[...].astype(F32)
    vc = vc_ref[...]
    s0_f = _dot_tn((kc * jnp.exp(lgf * (CTX_LEN - 1 - pc))).astype(BF16), vc)
    s0_b = _dot_tn((kc * jnp.exp(lgb * pc)).astype(BF16), vc)

    def chunk(n):
        return pl.ds(pl.multiple_of(n * c, c), c)

    def bwd_one(n, s):
        sb_ref[n] = s.astype(BF16)
        kt = k_ref[chunk(n), :].astype(F32).T
        v = v_ref[chunk(n), :]
        kvf_ref[n] = _dot((kt * zeta_f).astype(BF16), v)
        return g_b * s + _dot((kt * zeta_b).astype(BF16), v)

    def bwd(t, s):
        for u in range(RET_UNROLL):
            s = bwd_one(N_CHUNKS - 1 - (t * RET_UNROLL + u), s)
        return s

    lax.fori_loop(0, N_CHUNKS // RET_UNROLL, bwd, s0_b)

    n_groups = N_CHUNKS // RET_GROUP

    def fwd_scores(t, a_ref):
        for u in range(RET_GROUP):
            n = t * RET_GROUP + u
            a_ref[u] = _dot_nt(q_ref[chunk(n), :], k_ref[chunk(n), :])

    def fwd_out(t, a_ref, s):
        for u in range(RET_GROUP):
            n = t * RET_GROUP + u
            q = q_ref[chunk(n), :]
            scores = (a_ref[u] * decay).astype(BF16)
            cross = _dot(q, jnp.concatenate([s.astype(BF16), sb_ref[n]], axis=1))
            o = (_dot(scores, v_ref[chunk(n), :])
                 + cross[:, :RET_V_DIM] * xi_f
                 + cross[:, RET_V_DIM:] * xi_b)
            mu = jnp.mean(o, axis=-1, keepdims=True)
            oc = o - mu
            var = jnp.mean(oc * oc, axis=-1, keepdims=True)
            gn = oc * lax.rsqrt(var + GN_EPS)
            o_ref[chunk(n), :] = (g_ref[chunk(n), :].astype(F32) * gn).astype(BF16)
            s = g_f * s + kvf_ref[n]
        return s

    fwd_scores(0, a0_ref)

    def two_groups(i, s):
        fwd_scores(2 * i + 1, a1_ref)
        s = fwd_out(2 * i, a0_ref, s)
        fwd_scores(2 * i + 2, a0_ref)
        return fwd_out(2 * i + 1, a1_ref, s)

    s = lax.fori_loop(0, n_groups // 2 - 1, two_groups, s0_f)
    fwd_scores(n_groups - 1, a1_ref)
    s = fwd_out(n_groups - 2, a0_ref, s)
    fwd_out(n_groups - 1, a1_ref, s)


def _ret(log_gamma, rproj, proj, gr, cproj):
    qk = lambda off: off // RET_QK_DIM
    vv = lambda off: off // RET_V_DIM
    return pl.pallas_call(
        _ret_kernel,
        out_shape=jax.ShapeDtypeStruct((N_LAT, RET_V_W), BF16),
        grid=(BATCH, RET_HEADS),
        in_specs=[
            pl.BlockSpec(memory_space=pltpu.SMEM),
            pl.BlockSpec((SEQ, RET_QK_DIM), lambda b, h: (b, qk(ROPE_QR) + h)),
            pl.BlockSpec((SEQ, RET_QK_DIM), lambda b, h: (b, qk(ROPE_KR) + h)),
            pl.BlockSpec((SEQ, RET_V_DIM), lambda b, h: (b, vv(PLAIN_VR) + h)),
            pl.BlockSpec((SEQ, RET_V_DIM), lambda b, h: (b, h)),
            pl.BlockSpec((CTX_LEN, RET_QK_DIM), lambda b, h: (b, qk(OFF_KR) + h)),
            pl.BlockSpec((CTX_LEN, RET_V_DIM), lambda b, h: (b, vv(OFF_VR) + h)),
        ],
        out_specs=pl.BlockSpec((SEQ, RET_V_DIM), lambda b, h: (b, h)),
        scratch_shapes=[pltpu.VMEM((N_CHUNKS, RET_QK_DIM, RET_V_DIM), BF16),
                        pltpu.VMEM((N_CHUNKS, RET_QK_DIM, RET_V_DIM), F32),
                        pltpu.VMEM((RET_GROUP, RET_CHUNK, RET_CHUNK), F32),
                        pltpu.VMEM((RET_GROUP, RET_CHUNK, RET_CHUNK), F32)],
        compiler_params=_params("parallel", "parallel"),
        name="retention",
    )(log_gamma, rproj, rproj, proj, gr, cproj, cproj)


def _merge_kernel(ona_ref, ret_ref, gna_ref, gret_ref, x_ref, mod_ref,
                  wna_ref, wret_ref, wout_ref, lng_ref, lnb_ref, *rest, n_cast):
    o_ref = rest[n_cast]
    _cast_blocks(rest[:n_cast], rest[n_cast + 1:])
    y_na = _dot(ona_ref[...], wna_ref[...])
    y_ret = _dot(ret_ref[...], wret_ref[...])
    z = (gna_ref[...].astype(F32) * y_na + gret_ref[...].astype(F32) * y_ret).astype(BF16)
    y = _dot(z, wout_ref[...])
    g = mod_ref[5:6, :]
    o_ref[...] = _layer_norm(ALPHA * x_ref[...] + g * y, lng_ref[...], lnb_ref[...], LN_EPS)


def _merge(o_na, ret, gates, x, mod, w_na, w_ret, w_out, ln_g, ln_b, *, cast=()):
    tm = MERGE_ROWS
    tiles_per_batch = SEQ // tm
    n_steps = N_LAT // tm
    resident = lambda shape: pl.BlockSpec(shape, lambda i: (0, 0), pipeline_mode=pl.Buffered(1))
    cast = tuple(cast)
    cast_specs = [_cast_specs(job, lambda i: i, n_steps) for job in cast]
    out = pl.pallas_call(
        functools.partial(_merge_kernel, n_cast=len(cast)),
        out_shape=[jax.ShapeDtypeStruct((N_LAT, D_MODEL), F32)] + [_cast_out_shape(job) for job in cast],
        grid=(n_steps,),
        in_specs=[
            pl.BlockSpec((tm, NA_W), lambda i: (i, 0)),
            pl.BlockSpec((tm, RET_V_W), lambda i: (i, 0)),
            pl.BlockSpec((tm, D_MODEL), lambda i: (i, 0)),
            pl.BlockSpec((tm, D_MODEL), lambda i: (i, 1)),
            pl.BlockSpec((tm, D_MODEL), lambda i: (i, 0)),
            pl.BlockSpec((None, N_MOD, D_MODEL), lambda i: (i // tiles_per_batch, 0, 0)),
            resident((NA_W, D_MODEL)),
            resident((RET_V_W, D_MODEL)),
            resident((D_MODEL, D_MODEL)),
            pl.BlockSpec((1, D_MODEL), lambda i: (0, 0)),
            pl.BlockSpec((1, D_MODEL), lambda i: (0, 0)),
        ] + [src for src, _ in cast_specs],
        out_specs=[pl.BlockSpec((tm, D_MODEL), lambda i: (i, 0))] + [dst for _, dst in cast_specs],
        compiler_params=_params("arbitrary"),
        name="merge",
    )(o_na, ret, gates, gates, x, mod, w_na, w_ret, w_out, ln_g, ln_b, *[job.w for job in cast])
    return out if cast else out[0]


def _rope_tables():
    nfreq = RET_QK_DIM // 4
    pos = jnp.arange(SEQ)
    prow = (pos // GRID_W).astype(F32)
    pcol = (pos % GRID_W).astype(F32)
    freqs = ROPE_BASE ** (-jnp.arange(nfreq, dtype=F32) / nfreq)
    ang = jnp.stack([prow, pcol], axis=-1)[:, :, None] * freqs
    cos = jnp.cos(ang)
    sin = jnp.sin(ang)
    cos_t = jnp.stack([cos, cos], axis=2).reshape(SEQ, RET_QK_DIM)
    sin_t = jnp.stack([-sin, sin], axis=2).reshape(SEQ, RET_QK_DIM)
    return cos_t, sin_t


def _na_bias_table(rpb):
    wc = NA_WIN_COLS
    n_dc = 2 * wc - 1
    col = jnp.arange(GRID_W)
    col_start = jnp.clip(col - wc // 2, 0, GRID_W - wc)
    col_mask = (col[None, :] >= col_start[:, None]) & (col[None, :] < col_start[:, None] + wc)
    dc_idx = jnp.clip(col[None, :] - col[:, None] + wc - 1, 0, n_dc - 1)
    onehot = ((dc_idx[None] == jnp.arange(n_dc)[:, None, None]) & col_mask[None]).astype(F32)
    t = jnp.einsum('hdc,cqk->hdqk', rpb, onehot, precision=lax.Precision.HIGHEST)
    t = t + jnp.where(col_mask, 0.0, NEG_INF)
    n_dr = 2 * NA_WIN_ROWS - 1
    t = t.reshape(NA_HEADS // 2, 2, n_dr, GRID_W, GRID_W).transpose(0, 2, 1, 3, 4)
    t = t.reshape(NA_HEADS // 2, n_dr, 2 * GRID_W, GRID_W)
    return jnp.concatenate([t[:, :-1], t[:, 1:]], axis=-1)


def _column_scale(n_cols, lo, hi, value):
    col = jnp.arange(n_cols)
    return jnp.where((col >= lo) & (col < hi), value, 1.0).astype(F32).reshape(1, n_cols)


def kernel(x, c, ctx, c_ctx, w_ada, b_ada, ln_g, ln_b, ffn1_w_in, ffn1_w_out, ffn2_w_in, ffn2_w_out,
           w_mix_in, na_rpb, ret_decay_fwd, ret_decay_bwd, w_na_out, w_ret_out, w_mix_out):
    assert x.shape == (BATCH, SEQ, D_MODEL) and ctx.shape == (BATCH, CTX_LEN, D_MODEL)
    assert w_ada.shape == (DEPTH, D_MODEL, N_MOD * D_MODEL) and w_mix_in.shape == (DEPTH, D_MODEL, MIX_COLS)
    l = 0
    ctx_mod = BATCH
    cc = jnp.zeros((MOD_ROWS, D_MODEL), F32).at[:BATCH].set(c).at[ctx_mod].set(c_ctx)
    mod = _adaln(cc, w_ada[l], b_ada[l].reshape(1, -1)).reshape(MOD_ROWS, N_MOD, D_MODEL)

    row = lambda v: v.reshape(1, D_MODEL)
    xf = x.reshape(N_LAT, D_MODEL)
    cf = ctx.reshape(N_CTX, D_MODEL)
    w_na, w_ret, w_out = w_na_out[l].astype(BF16), w_ret_out[l].astype(BF16), w_mix_out[l].astype(BF16)
    _, hc, w1_in, w1_out = _ffn_convert(cf, mod, ffn1_w_in[l], ffn1_w_out[l], row(ln_g[l, 0]), row(ln_b[l, 0]),
                                        k0=0, mod_base=ctx_mod)
    x1, h1, w_mix = _ffn(xf, mod, w1_in, w1_out, row(ln_g[l, 0]), row(ln_b[l, 0]), k0=0, tm=FFN_ROWS,
                         rows_per_mod=SEQ, mod_base=0, mix_out=True,
                         cast=[_CastJob(w_mix_in[l], CAST_ROWS_FFN)])

    cos_t, sin_t = _rope_tables()
    rproj = _rope_proj(h1, w_mix, cos_t, sin_t)
    latent = dict(tm=PROJ_ROWS, tn=PROJ_COLS)
    ones = jnp.ones((1, GATE_COLS), F32)
    proj = _proj(h1, w_mix, _column_scale(PLAIN_COLS, PLAIN_QA, PLAIN_COLS, NA_HEAD_DIM ** -0.5),
                 kind=KIND_SCALE, n_cols=PLAIN_COLS,
                 w_block=lambda j: j + (j >= OFF_KR // PROJ_COLS).astype(jnp.int32), **latent)
    gr = _proj(h1, w_mix, ones, kind=KIND_SILU, n_cols=RET_V_W,
               w_block=lambda j: OFF_GR // PROJ_COLS + j, **latent)
    gates = _proj(h1, w_mix, ones, kind=KIND_SIGMOID, n_cols=GATE_COLS,
                  w_block=lambda j: OFF_GNA // PROJ_COLS + j, **latent)
    cproj = _proj(hc, w_mix, _column_scale(KV_COLS, OFF_KR, OFF_VR, RET_QK_DIM ** -0.5),
                  kind=KIND_SCALE, n_cols=KV_COLS, tm=N_CTX, tn=CTX_PROJ_COLS, w_block=lambda j: j)

    o_na = _na(proj, cproj, _na_bias_table(na_rpb[l]))
    log_gamma = jnp.stack([jax.nn.log_sigmoid(ret_decay_fwd[l].astype(F32)),
                           jax.nn.log_sigmoid(ret_decay_bwd[l].astype(F32))])
    ret = _ret(log_gamma, rproj, proj, gr, cproj)

    x2, w2_in, w2_out = _merge(o_na, ret, gates, x1, mod, w_na, w_ret, w_out, row(ln_g[l, 1]), row(ln_b[l, 1]),
                               cast=[_CastJob(ffn2_w_in[l], CAST_ROWS_MERGE_IN, swiglu=True),
                                     _CastJob(ffn2_w_out[l], CAST_ROWS_MERGE_OUT)])
    x3 = _ffn(x2, mod, w2_in, w2_out, row(ln_g[l, 2]), row(ln_b[l, 2]), k0=6, tm=FFN2_ROWS,
              rows_per_mod=SEQ, mod_base=0, keep_h=False, vmem_limit=FFN2_VMEM_LIMIT_BYTES)
    return x3.reshape(BATCH, SEQ, D_MODEL)
```
